```python
import jax, jax.numpy as jnp
from jax import lax
import numpy as np

D_MODEL = 4096
BATCH = 2
SEQ = 8192
DEPTH = 1

NORM_EPS = 1e-6
DN_HEADS = 16
DN_HEAD_DIM = D_MODEL // 32
DN_WIDTH = DN_HEADS * DN_HEAD_DIM
CONV_WIDTH = 4
CHUNK = 64
SWA_Q_HEADS = 16
SWA_KV_HEADS = 4
SWA_HEAD_DIM = D_MODEL // 32
SWA_Q_WIDTH = SWA_Q_HEADS * SWA_HEAD_DIM
SWA_KV_WIDTH = SWA_KV_HEADS * SWA_HEAD_DIM
WINDOW = 128
BLOCK = 128
MIX_WIDTH = DN_WIDTH + SWA_Q_WIDTH
IN_PROJ_WIDTH = 4 * DN_WIDTH + 2 * DN_HEADS + SWA_Q_WIDTH + 2 * SWA_KV_WIDTH
FFN_HIDDEN = -(-8 * D_MODEL // (3 * 256)) * 256

kernel_name = "hybrid_gdn_swa_parallel_heads"


def rms_norm(x, w):
    xf = x.astype(jnp.float32)
    y = xf * lax.rsqrt(jnp.mean(xf * xf, axis=-1, keepdims=True) + NORM_EPS)
    return y * w.astype(jnp.float32)


def l2_norm(x):
    return x * lax.rsqrt(jnp.sum(x * x, axis=-1, keepdims=True) + NORM_EPS)


def alibi_slopes(n_heads):
    return jnp.exp2(-8.0 * jnp.arange(1, n_heads + 1, dtype=jnp.float32) / n_heads)


def causal_conv_silu(x, w):
    k = w.shape[0]
    y = lax.conv_general_dilated(
        x, w[:, None, :].astype(x.dtype), window_strides=(1,), padding=[(k - 1, 0)],
        dimension_numbers=("NWC", "WIO", "NWC"), feature_group_count=x.shape[-1])
    return jax.nn.silu(y)


def chunk_gated_delta_rule(q, k, v, g, beta):
    b, t, h, d = q.shape
    n = t // CHUNK

    def to_chunks(a):
        a = a.reshape((b, n, CHUNK, h) + a.shape[3:])
        return jnp.moveaxis(a, 3, 1)

    q, k, v, g, beta = (to_chunks(a) for a in (q, k, v, g, beta))
    gc = jnp.cumsum(g, axis=-1)
    idx = jnp.arange(CHUNK)
    incl = idx[:, None] >= idx[None, :]
    strict = idx[:, None] > idx[None, :]
    decay = jnp.exp(jnp.where(incl, gc[..., :, None] - gc[..., None, :], -jnp.inf))

    kb = k * beta[..., None]
    lower = jnp.where(strict, jnp.einsum("bhncd,bhnsd->bhncs", kb, k) * decay, 0.0)
    a_mat = lower + jnp.eye(CHUNK, dtype=jnp.float32)
    solve = lambda rhs: lax.linalg.triangular_solve(
        a_mat, rhs, left_side=True, lower=True, unit_diagonal=True)
    u = solve(v * beta[..., None])
    w = solve(kb * jnp.exp(gc)[..., None])

    qk = jnp.einsum("bhncd,bhnsd->bhncs", q, k) * decay
    q_dec = q * jnp.exp(gc)[..., None]
    k_tail = k * jnp.exp(gc[..., -1:] - gc)[..., None]
    g_last = jnp.exp(gc[..., -1])

    xs = tuple(jnp.moveaxis(a, 2, 0) for a in (q_dec, qk, u, w, k_tail, g_last))

    def step(state, inp):
        qd, qk_i, u_i, w_i, kt, gl = inp
        v_new = u_i - jnp.einsum("bhcd,bhde->bhce", w_i, state)
        o = jnp.einsum("bhcd,bhde->bhce", qd, state) + jnp.einsum("bhcs,bhse->bhce", qk_i, v_new)
        state = state * gl[..., None, None] + jnp.einsum("bhcd,bhce->bhde", kt, v_new)
        return state, o

    s0 = jnp.zeros((b, h, d, d), jnp.float32)
    _, o = lax.scan(step, s0, xs)
    return o.transpose(1, 0, 3, 2, 4).reshape(b, t, h, d)


def gated_deltanet(q, k, v, z, a, bt, conv_w, a_log, dt_bias, norm_w):
    bsz, t, _ = q.shape
    qkv = causal_conv_silu(jnp.concatenate([q, k, v], axis=-1), conv_w)
    q, k, v = jnp.split(qkv.astype(jnp.float32), [DN_WIDTH, 2 * DN_WIDTH], axis=-1)
    shp = (bsz, t, DN_HEADS, DN_HEAD_DIM)
    q = l2_norm(q.reshape(shp)) * (DN_HEAD_DIM ** -0.5)
    k = l2_norm(k.reshape(shp))
    v = v.reshape(shp)
    beta = jax.nn.sigmoid(bt.astype(jnp.float32))
    g = -jnp.exp(a_log.astype(jnp.float32)) * jax.nn.softplus(
        a.astype(jnp.float32) + dt_bias.astype(jnp.float32))
    o = chunk_gated_delta_rule(q, k, v, g, beta)
    o = rms_norm(o, norm_w) * jax.nn.silu(z.astype(jnp.float32).reshape(shp))
    return o.reshape(bsz, t, DN_WIDTH)


def sliding_window_gqa(q, k, v, q_norm_w, k_norm_w, sinks):
    bsz, t, _ = q.shape
    grp = SWA_Q_HEADS // SWA_KV_HEADS
    nb = t // BLOCK
    q = rms_norm(q.reshape(bsz, t, SWA_KV_HEADS, grp, SWA_HEAD_DIM), q_norm_w)
    k = rms_norm(k.reshape(bsz, t, SWA_KV_HEADS, SWA_HEAD_DIM), k_norm_w)
    v = v.astype(jnp.float32).reshape(bsz, t, SWA_KV_HEADS, SWA_HEAD_DIM)

    qb = q.reshape(bsz, nb, BLOCK, SWA_KV_HEADS, grp, SWA_HEAD_DIM)

    def with_prev(a):
        a = a.reshape(bsz, nb, BLOCK, SWA_KV_HEADS, SWA_HEAD_DIM)
        prev = jnp.concatenate([jnp.zeros_like(a[:, :1]), a[:, :-1]], axis=1)
        return jnp.concatenate([prev, a], axis=2)

    kk, vv = with_prev(k), with_prev(v)
    s = jnp.einsum("bnqhgd,bnkhd->bnhgqk", qb, kk) * (SWA_HEAD_DIM ** -0.5)

    qpos = jnp.arange(BLOCK) + BLOCK
    kpos = jnp.arange(2 * BLOCK)
    dist = (qpos[:, None] - kpos[None, :])
    key_global = jnp.arange(nb)[:, None] * BLOCK + kpos[None, :] - BLOCK
    valid = ((dist >= 0) & (dist < WINDOW))[None] & (key_global >= 0)[:, None, :]

    slopes = alibi_slopes(SWA_Q_HEADS).reshape(SWA_KV_HEADS, grp)
    s = s - slopes[:, :, None, None] * dist.astype(jnp.float32)
    s = jnp.where(valid[None, :, None, None], s, -jnp.inf)
    sink = jnp.broadcast_to(
        sinks.astype(jnp.float32).reshape(SWA_KV_HEADS, grp)[:, :, None, None],
        s.shape[:-1] + (1,))
    p = jax.nn.softmax(jnp.concatenate([s, sink], axis=-1), axis=-1)[..., :-1]
    o = jnp.einsum("bnhgqk,bnkhd->bnqhgd", p, vv)
    return o.reshape(bsz, t, SWA_Q_WIDTH)


def in_proj_cuts():
    sizes = [DN_WIDTH] * 4 + [DN_HEADS] * 2 + [SWA_Q_WIDTH, SWA_KV_WIDTH, SWA_KV_WIDTH]
    return np.cumsum(sizes)[:-1].tolist()


def setup_inputs(seed: int = 0) -> dict:
    key = jax.random.key(seed)
    ks = jax.random.split(key, 16)
    f32 = jnp.float32
    nrm = lambda k, shape, scale: jax.random.normal(k, shape, f32) * scale
    x = jax.random.normal(ks[0], (BATCH, SEQ, D_MODEL), f32)
    attn_norm_w = 1.0 + nrm(ks[1], (DEPTH, D_MODEL), 0.02)
    w_in = nrm(ks[2], (DEPTH, D_MODEL, IN_PROJ_WIDTH), D_MODEL ** -0.5)
    conv_w = nrm(ks[3], (DEPTH, CONV_WIDTH, 3 * DN_WIDTH), CONV_WIDTH ** -0.5)
    a_log = jnp.log(jax.random.uniform(ks[4], (DEPTH, DN_HEADS), f32, 1.0, 16.0))
    dt = jnp.exp(jax.random.uniform(ks[5], (DEPTH, DN_HEADS), f32, np.log(1e-3), np.log(1e-1)))
    dt_bias = dt + jnp.log(-jnp.expm1(-dt))
    dn_norm_w = 1.0 + nrm(ks[6], (DEPTH, DN_HEAD_DIM), 0.02)
    q_norm_w = 1.0 + nrm(ks[7], (DEPTH, SWA_HEAD_DIM), 0.02)
    k_norm_w = 1.0 + nrm(ks[8], (DEPTH, SWA_HEAD_DIM), 0.02)
    sinks = nrm(ks[9], (DEPTH, SWA_Q_HEADS), 0.5)
    w_out = nrm(ks[10], (DEPTH, MIX_WIDTH, D_MODEL), MIX_WIDTH ** -0.5)
    ffn_norm_w = 1.0 + nrm(ks[11], (DEPTH, D_MODEL), 0.02)
    w_gate = nrm(ks[12], (DEPTH, D_MODEL, FFN_HIDDEN), D_MODEL ** -0.5)
    w_up = nrm(ks[13], (DEPTH, D_MODEL, FFN_HIDDEN), D_MODEL ** -0.5)
    w_down = nrm(ks[14], (DEPTH, FFN_HIDDEN, D_MODEL), FFN_HIDDEN ** -0.5)
    return {"x": x, "attn_norm_w": attn_norm_w, "w_in": w_in, "conv_w": conv_w,
            "a_log": a_log, "dt_bias": dt_bias, "dn_norm_w": dn_norm_w,
            "q_norm_w": q_norm_w, "k_norm_w": k_norm_w, "sinks": sinks, "w_out": w_out,
            "ffn_norm_w": ffn_norm_w, "w_gate": w_gate, "w_up": w_up, "w_down": w_down}


def reference(x, attn_norm_w, w_in, conv_w, a_log, dt_bias, dn_norm_w, q_norm_w, k_norm_w,
              sinks, w_out, ffn_norm_w, w_gate, w_up, w_down):
    dtype = x.dtype
    cuts = in_proj_cuts()
    for l in range(DEPTH):
        h = rms_norm(x, attn_norm_w[l]).astype(dtype)
        proj = jnp.einsum("btd,de->bte", h, w_in[l])
        dq, dk, dv, dz, da, db, sq, sk, sv = jnp.split(proj, cuts, axis=-1)
        o_dn = gated_deltanet(dq, dk, dv, dz, da, db, conv_w[l], a_log[l], dt_bias[l], dn_norm_w[l])
        o_sw = sliding_window_gqa(sq, sk, sv, q_norm_w[l], k_norm_w[l], sinks[l])
        mixed = jnp.concatenate([o_dn, o_sw], axis=-1).astype(dtype)
        x = x + jnp.einsum("bte,ed->btd", mixed, w_out[l])
        h = rms_norm(x, ffn_norm_w[l]).astype(dtype)
        gate = jnp.einsum("btd,df->btf", h, w_gate[l])
        up = jnp.einsum("btd,df->btf", h, w_up[l])
        x = x + jnp.einsum("btf,fd->btd", jax.nn.silu(gate) * up, w_down[l])
    return x
```

```python
import functools

import jax
import jax.numpy as jnp
from jax import lax
from jax.experimental import pallas as pl
from jax.experimental.pallas import tpu as pltpu

NORM_EPS = 1e-6
HEAD_DIM = 128
DN_HEADS = 16
CHUNK = 64
CONV_WIDTH = 4
SWA_Q_HEADS = 16
SWA_KV_HEADS = 4
SWA_GROUP = SWA_Q_HEADS // SWA_KV_HEADS
WINDOW = 128
LANES = 128
VMEM_LIMIT_BYTES = 56 * 1024 * 1024

BF16 = jnp.bfloat16
F32 = jnp.float32


def _params(*semantics):
    return pltpu.CompilerParams(dimension_semantics=semantics, vmem_limit_bytes=VMEM_LIMIT_BYTES)


def _rmsnorm_kernel(x_ref, w_ref, o_ref):
    x = x_ref[...]
    ms = jnp.mean(x * x, axis=-1, keepdims=True)
    o_ref[...] = (x * lax.rsqrt(ms + NORM_EPS) * w_ref[...]).astype(o_ref.dtype)


def _rmsnorm(x, w, tm=256):
    m, d = x.shape
    tm = min(tm, m)
    return pl.pallas_call(
        _rmsnorm_kernel,
        grid=(m // tm,),
        in_specs=[pl.BlockSpec((tm, d), lambda i: (i, 0)), pl.BlockSpec((1, d), lambda i: (0, 0))],
        out_specs=pl.BlockSpec((tm, d), lambda i: (i, 0)),
        out_shape=jax.ShapeDtypeStruct((m, d), BF16),
        compiler_params=_params("parallel"),
        name="rmsnorm",
    )(x, w.reshape(1, d))


def _mm_kernel(a_ref, w_ref, *rest, nk, residual):
    if residual:
        r_ref, o_ref, acc_ref = rest
    else:
        o_ref, acc_ref = rest
    k = pl.program_id(2)
    part = jnp.dot(a_ref[...], w_ref[...], preferred_element_type=F32)

    @pl.when(k == 0)
    def _():
        acc_ref[...] = part

    @pl.when(k > 0)
    def _():
        acc_ref[...] += part

    @pl.when(k == nk - 1)
    def _():
        out = acc_ref[...]
        if residual:
            out = out + r_ref[...]
        o_ref[...] = out.astype(o_ref.dtype)


def _matmul(a, w, *, tm, tn, tk, out_dtype, residual=None, name):
    m, kd = a.shape
    n = w.shape[1]
    tm = min(tm, m)
    nk = kd // tk
    in_specs = [pl.BlockSpec((tm, tk), lambda i, j, k: (i, k)),
                pl.BlockSpec((tk, tn), lambda i, j, k: (k, j))]
    args = [a, w]
    if residual is not None:
        in_specs.append(pl.BlockSpec((tm, tn), lambda i, j, k: (i, j)))
        args.append(residual)
    return pl.pallas_call(
        functools.partial(_mm_kernel, nk=nk, residual=residual is not None),
        grid=(m // tm, n // tn, nk),
        in_specs=in_specs,
        out_specs=pl.BlockSpec((tm, tn), lambda i, j, k: (i, j)),
        out_shape=jax.ShapeDtypeStruct((m, n), out_dtype),
        scratch_shapes=[pltpu.VMEM((tm, tn), F32)],
        compiler_params=_params("parallel", "parallel", "arbitrary"),
        name=name,
    )(*args)


def _gate_up_kernel(a_ref, wg_ref, wu_ref, o_ref):
    a = a_ref[...]
    g = jnp.dot(a, wg_ref[...], preferred_element_type=F32)
    u = jnp.dot(a, wu_ref[...], preferred_element_type=F32)
    o_ref[...] = (g * jax.nn.sigmoid(g) * u).astype(o_ref.dtype)


def _gate_up(a, wg, wu, *, tm, tn):
    m, kd = a.shape
    n = wg.shape[1]
    tm = min(tm, m)
    return pl.pallas_call(
        _gate_up_kernel,
        grid=(m // tm, n // tn),
        in_specs=[pl.BlockSpec((tm, kd), lambda i, j: (i, 0)),
                  pl.BlockSpec((kd, tn), lambda i, j: (0, j)),
                  pl.BlockSpec((kd, tn), lambda i, j: (0, j))],
        out_specs=pl.BlockSpec((tm, tn), lambda i, j: (i, j)),
        out_shape=jax.ShapeDtypeStruct((m, n), BF16),
        compiler_params=_params("parallel", "parallel"),
        name="gate_up",
    )(a, wg, wu)


def _bdot(a, b):
    return jnp.dot(a.astype(BF16), b.astype(BF16), preferred_element_type=F32)


def _bdot_nt(a, b):
    return lax.dot_general(a.astype(BF16), b.astype(BF16), (((1,), (1,)), ((), ())),
                           preferred_element_type=F32)


def _bdot_tn(a, b):
    return lax.dot_general(a.astype(BF16), b.astype(BF16), (((0,), (0,)), ((), ())),
                           preferred_element_type=F32)


def _xdot(a, b):
    return jnp.dot(a, b, preferred_element_type=F32, precision=lax.Precision.HIGHEST)


def _gdn_kernel(q_ref, k_ref, v_ref, z_ref, ab_ref, cwq_ref, cwk_ref, cwv_ref,
                alog_ref, dtb_ref, normw_ref, o_ref, xbuf, ybuf, s_ref, *, hb, tb):
    hg = pl.program_id(1)
    t = pl.program_id(2)
    c = hb * HEAD_DIM
    pad = 8

    @pl.when(t == 0)
    def _():
        xbuf[0:pad, :] = jnp.zeros((pad, 3 * c), F32)
        s_ref[...] = jnp.zeros_like(s_ref)

    xbuf[pad:pad + tb, 0:c] = q_ref[...]
    xbuf[pad:pad + tb, c:2 * c] = k_ref[...]
    xbuf[pad:pad + tb, 2 * c:3 * c] = v_ref[...]

    cw = jnp.concatenate([cwq_ref[...], cwk_ref[...], cwv_ref[...]], axis=1)
    acc = None
    for j in range(CONV_WIDTH):
        start = pad - (CONV_WIDTH - 1) + j
        term = xbuf[start:start + tb, :] * cw[j:j + 1, :]
        acc = term if acc is None else acc + term
    ybuf[...] = acc * jax.nn.sigmoid(acc)
    xbuf[0:pad, :] = xbuf[tb:tb + pad, :]

    ri = lax.broadcasted_iota(jnp.int32, (CHUNK, CHUNK), 0)
    ci = lax.broadcasted_iota(jnp.int32, (CHUNK, CHUNK), 1)
    incl = ri >= ci
    strict = ri > ci
    eye = ri == ci
    tri_ones = jnp.where(incl, 1.0, 0.0).astype(F32)
    all_ones = jnp.ones((CHUNK, CHUNK), F32)
    level_masks = []
    for lvl in range(CHUNK.bit_length() - 1):
        same_pair = (ri >> (lvl + 1)) == (ci >> (lvl + 1))
        level_masks.append(same_pair & (((ri >> lvl) & 1) == 1) & (((ci >> lvl) & 1) == 0))
    lane = lax.broadcasted_iota(jnp.int32, (CHUNK, LANES), 1)
    alog = alog_ref[...]
    dtb = dtb_ref[...]
    normw = normw_ref[...]
    scale = HEAD_DIM ** -0.5

    def chunk_body(ch, carry):
        r0 = pl.multiple_of(ch * CHUNK, CHUNK)
        rows = pl.ds(r0, CHUNK)
        ab = ab_ref[rows, :]
        g_all = -jnp.exp(alog) * jax.nn.softplus(ab + dtb)
        beta_all = jax.nn.sigmoid(ab)
        gc_all = _xdot(tri_ones, g_all)
        for hh in range(hb):
            head = hg * hb + hh
            gc = jnp.sum(jnp.where(lane == head, gc_all, 0.0), axis=-1, keepdims=True)
            beta = jnp.sum(jnp.where(lane == head + DN_HEADS, beta_all, 0.0), axis=-1, keepdims=True)
            gc_last = gc[CHUNK - 1:CHUNK, :]
            gc_row = _xdot(all_ones, jnp.where(eye, gc, 0.0))
            decay = jnp.exp(jnp.where(incl, gc - gc_row, -jnp.inf))

            lo = hh * HEAD_DIM
            qh = ybuf[rows, lo:lo + HEAD_DIM]
            kh = ybuf[rows, c + lo:c + lo + HEAD_DIM]
            vh = ybuf[rows, 2 * c + lo:2 * c + lo + HEAD_DIM]
            qn = qh * lax.rsqrt(jnp.sum(qh * qh, axis=-1, keepdims=True) + NORM_EPS) * scale
            kn = kh * lax.rsqrt(jnp.sum(kh * kh, axis=-1, keepdims=True) + NORM_EPS)

            sc = _bdot_nt(jnp.concatenate([qn, kn], axis=0), kn)
            qk = sc[0:CHUNK] * decay
            a_mat = jnp.where(strict, sc[CHUNK:2 * CHUNK] * beta * decay, 0.0)

            x_inv = jnp.where(eye, 1.0, 0.0) - jnp.where(level_masks[0], a_mat, 0.0)
            for mask in level_masks[1:]:
                a_lvl = jnp.where(mask, a_mat, 0.0)
                x_inv = x_inv - _bdot(x_inv, _bdot(a_lvl, x_inv))

            eg = jnp.exp(gc)
            kb = kn * beta
            uw = _bdot(x_inv, jnp.concatenate([vh * beta, kb * eg], axis=1))
            u = uw[:, 0:HEAD_DIM]
            w = uw[:, HEAD_DIM:2 * HEAD_DIM]
            qd = qn * eg
            kt = kn * jnp.exp(gc_last - gc)
            gl = jnp.exp(gc_last)

            state = s_ref[hh]
            ws = _bdot(jnp.concatenate([w, qd], axis=0), state)
            v_new = u - ws[0:CHUNK]
            o = ws[CHUNK:2 * CHUNK] + _bdot(qk, v_new)
            s_ref[hh] = state * gl + _bdot_tn(kt, v_new)

            zh = z_ref[rows, lo:lo + HEAD_DIM]
            o = o * lax.rsqrt(jnp.mean(o * o, axis=-1, keepdims=True) + NORM_EPS) * normw
            o_ref[rows, lo:lo + HEAD_DIM] = (o * (zh * jax.nn.sigmoid(zh))).astype(o_ref.dtype)
        return carry

    lax.fori_loop(0, tb // CHUNK, chunk_body, 0)


def _gdn(proj, ab, conv_w, alog_lane, dtb_lane, norm_w, *, batch, seq, hb, tb):
    m = proj.shape[0]
    c = hb * HEAD_DIM
    width = DN_HEADS * HEAD_DIM
    nt = seq // tb
    off = width // c

    def col(section):
        return lambda b, hg, t: (b * nt + t, section * off + hg)

    def wcol(section):
        return lambda b, hg, t: (0, section * off + hg)

    lane_spec = pl.BlockSpec((1, LANES), lambda b, hg, t: (0, 0))
    return pl.pallas_call(
        functools.partial(_gdn_kernel, hb=hb, tb=tb),
        grid=(batch, DN_HEADS // hb, nt),
        in_specs=[pl.BlockSpec((tb, c), col(0)), pl.BlockSpec((tb, c), col(1)),
                  pl.BlockSpec((tb, c), col(2)), pl.BlockSpec((tb, c), col(3)),
                  pl.BlockSpec((tb, LANES), lambda b, hg, t: (b * nt + t, 0)),
                  pl.BlockSpec((CONV_WIDTH, c), wcol(0)), pl.BlockSpec((CONV_WIDTH, c), wcol(1)),
                  pl.BlockSpec((CONV_WIDTH, c), wcol(2)),
                  lane_spec, lane_spec, lane_spec],
        out_specs=pl.BlockSpec((tb, c), lambda b, hg, t: (b * nt + t, hg)),
        out_shape=jax.ShapeDtypeStruct((m, width), BF16),
        scratch_shapes=[pltpu.VMEM((tb + 8, 3 * c), F32), pltpu.VMEM((tb, 3 * c), F32),
                        pltpu.VMEM((hb, HEAD_DIM, HEAD_DIM), F32)],
        compiler_params=_params("parallel", "parallel", "arbitrary"),
        name="gated_delta_rule",
    )(proj, proj, proj, proj, ab, conv_w, conv_w, conv_w, alog_lane, dtb_lane, norm_w)


def _swa_kernel(sinks_ref, q_ref, kp_ref, kc_ref, vp_ref, vc_ref, qw_ref, kw_ref, o_ref):
    kvh = pl.program_id(1)
    n = pl.program_id(2)
    blk = WINDOW
    rows = SWA_GROUP * blk

    def rms(x, w):
        return x * lax.rsqrt(jnp.mean(x * x, axis=-1, keepdims=True) + NORM_EPS) * w

    qw = qw_ref[...]
    kw = kw_ref[...]
    q = jnp.concatenate([rms(q_ref[:, g * HEAD_DIM:(g + 1) * HEAD_DIM], qw) for g in range(SWA_GROUP)], axis=0)
    k = jnp.concatenate([rms(kp_ref[...], kw), rms(kc_ref[...], kw)], axis=0)
    v = jnp.concatenate([vp_ref[...], vc_ref[...]], axis=0)

    s = _bdot_nt(q, k) * (HEAD_DIM ** -0.5)

    ri = lax.broadcasted_iota(jnp.int32, (rows, 2 * blk), 0)
    ki = lax.broadcasted_iota(jnp.int32, (rows, 2 * blk), 1)
    dist = (ri & (blk - 1)) + blk - ki
    valid = (dist >= 0) & (dist < WINDOW) & ((ki >= blk) | (n > 0))

    rcol = lax.broadcasted_iota(jnp.int32, (rows, 1), 0) >> (blk.bit_length() - 1)
    head = kvh * SWA_GROUP + rcol
    slope = jnp.exp2(-8.0 * (head + 1).astype(F32) / SWA_Q_HEADS)
    sink = jnp.zeros((rows, 1), F32)
    for g in range(SWA_GROUP):
        sink = jnp.where(rcol == g, sinks_ref[kvh * SWA_GROUP + g], sink)

    s = jnp.where(valid, s - slope * dist.astype(F32), -jnp.inf)
    mx = jnp.maximum(jnp.max(s, axis=-1, keepdims=True), sink)
    p = jnp.exp(s - mx)
    denom = jnp.sum(p, axis=-1, keepdims=True) + jnp.exp(sink - mx)
    o = _bdot(p, v) / denom
    for g in range(SWA_GROUP):
        o_ref[:, g * HEAD_DIM:(g + 1) * HEAD_DIM] = o[g * blk:(g + 1) * blk].astype(o_ref.dtype)


def _swa(proj, sinks, q_norm_w, k_norm_w, *, batch, seq, q_col, k_col, v_col):
    m = proj.shape[0]
    blk = WINDOW
    nb = seq // blk
    gw = SWA_GROUP * HEAD_DIM

    def cur(col):
        return lambda b, h, n: (b * nb + n, col + h)

    def prev(col):
        return lambda b, h, n: (b * nb + jnp.maximum(n - 1, 0), col + h)

    wspec = pl.BlockSpec((1, HEAD_DIM), lambda b, h, n: (0, 0))
    return pl.pallas_call(
        _swa_kernel,
        grid=(batch, SWA_KV_HEADS, nb),
        in_specs=[pl.BlockSpec(memory_space=pltpu.SMEM),
                  pl.BlockSpec((blk, gw), lambda b, h, n: (b * nb + n, q_col // SWA_GROUP + h)),
                  pl.BlockSpec((blk, HEAD_DIM), prev(k_col)), pl.BlockSpec((blk, HEAD_DIM), cur(k_col)),
                  pl.BlockSpec((blk, HEAD_DIM), prev(v_col)), pl.BlockSpec((blk, HEAD_DIM), cur(v_col)),
                  wspec, wspec],
        out_specs=pl.BlockSpec((blk, gw), lambda b, h, n: (b * nb + n, h)),
        out_shape=jax.ShapeDtypeStruct((m, SWA_Q_HEADS * HEAD_DIM), BF16),
        compiler_params=_params("parallel", "parallel", "arbitrary"),
        name="sliding_window_gqa",
    )(sinks, proj, proj, proj, proj, proj, q_norm_w, k_norm_w)


def _pad_lanes(v):
    return jnp.pad(v.astype(F32), (0, LANES - v.shape[0])).reshape(1, LANES)


def _layer(x2, batch, seq, attn_norm_w, w_in, conv_w, a_log, dt_bias, dn_norm_w, q_norm_w, k_norm_w,
           sinks, w_out, ffn_norm_w, w_gate, w_up, w_down):
    m, d = x2.shape
    dn_w = DN_HEADS * HEAD_DIM
    q_w = SWA_Q_HEADS * HEAD_DIM
    kv_w = SWA_KV_HEADS * HEAD_DIM
    gates_at = 4 * dn_w
    n_gates = 2 * DN_HEADS

    w_main = jnp.concatenate([w_in[:, :gates_at], w_in[:, gates_at + n_gates:]], axis=1).astype(BF16)
    w_ab = jnp.pad(w_in[:, gates_at:gates_at + n_gates], ((0, 0), (0, LANES - n_gates))).astype(BF16)
    f = w_gate.shape[1]
    f_pad = -(-f // 512) * 512
    wg = jnp.pad(w_gate, ((0, 0), (0, f_pad - f))).astype(BF16)
    wu = jnp.pad(w_up, ((0, 0), (0, f_pad - f))).astype(BF16)
    wd = jnp.pad(w_down, ((0, f_pad - f), (0, 0))).astype(BF16)
    wo = w_out.astype(BF16)

    h = _rmsnorm(x2, attn_norm_w)
    proj = _matmul(h, w_main, tm=1024, tn=512, tk=d, out_dtype=F32, name="in_proj")
    ab = _matmul(h, w_ab, tm=1024, tn=LANES, tk=d, out_dtype=F32, name="gate_proj")

    o_dn = _gdn(proj, ab, conv_w, _pad_lanes(a_log), _pad_lanes(dt_bias), dn_norm_w.reshape(1, HEAD_DIM),
                batch=batch, seq=seq, hb=4, tb=min(512, seq))
    sec = 4 * dn_w // HEAD_DIM
    o_sw = _swa(proj, sinks.astype(F32), q_norm_w.reshape(1, HEAD_DIM), k_norm_w.reshape(1, HEAD_DIM),
                batch=batch, seq=seq, q_col=sec, k_col=sec + q_w // HEAD_DIM,
                v_col=sec + (q_w + kv_w) // HEAD_DIM)

    mixed = jnp.concatenate([o_dn, o_sw], axis=1)
    x1 = _matmul(mixed, wo, tm=1024, tn=1024, tk=1024, out_dtype=F32, residual=x2, name="out_proj")

    h2 = _rmsnorm(x1, ffn_norm_w)
    act = _gate_up(h2, wg, wu, tm=1024, tn=512)
    return _matmul(act, wd, tm=1024, tn=1024, tk=f_pad // 4, out_dtype=F32, residual=x1, name="down_proj")


def kernel(x, attn_norm_w, w_in, conv_w, a_log, dt_bias, dn_norm_w, q_norm_w, k_norm_w, sinks, w_out,
           ffn_norm_w, w_gate, w_up, w_down):
    batch, seq, d = x.shape
    x2 = x.reshape(batch * seq, d)
    for l in range(attn_norm_w.shape[0]):
        x2 = _layer(x2, batch, seq, attn_norm_w[l], w_in[l], conv_w[l], a_log[l], dt_bias[l], dn_norm_w[l],
                    q_norm_w[l], k_norm_w[l], sinks[l], w_out[l], ffn_norm_w[l], w_gate[l], w_up[l], w_down[l])
    return x2.reshape(batch, seq, d)
```

```python
import functools

import jax
import jax.numpy as jnp
from jax import lax
from jax.experimental import pallas as pl
from jax.experimental.pallas import tpu as pltpu

NORM_EPS = 1e-6
HEAD_DIM = 128
DN_HEADS = 16
CHUNK = 64
CONV_WIDTH = 4
SWA_Q_HEADS = 16
SWA_KV_HEADS = 4
SWA_GROUP = SWA_Q_HEADS // SWA_KV_HEADS
WINDOW = 128
LANES = 128
VMEM_LIMIT_BYTES = 56 * 1024 * 1024

BF16 = jnp.bfloat16
F32 = jnp.float32


def _params(*semantics):
    return pltpu.CompilerParams(dimension_semantics=semantics, vmem_limit_bytes=VMEM_LIMIT_BYTES)


def _rmsnorm_kernel(x_ref, w_ref, o_ref):
    x = x_ref[...]
    ms = jnp.mean(x * x, axis=-1, keepdims=True)
    o_ref[...] = (x * lax.rsqrt(ms + NORM_EPS) * w_ref[...]).astype(o_ref.dtype)


def _rmsnorm(x, w, tm=256):
    m, d = x.shape
    tm = min(tm, m)
    return pl.pallas_call(
        _rmsnorm_kernel,
        grid=(m // tm,),
        in_specs=[pl.BlockSpec((tm, d), lambda i: (i, 0)), pl.BlockSpec((1, d), lambda i: (0, 0))],
        out_specs=pl.BlockSpec((tm, d), lambda i: (i, 0)),
        out_shape=jax.ShapeDtypeStruct((m, d), BF16),
        compiler_params=_params("parallel"),
        name="rmsnorm",
    )(x, w.reshape(1, d))


def _mm_kernel(a_ref, w_ref, *rest, nk, residual):
    if residual:
        r_ref, o_ref, acc_ref = rest
    else:
        o_ref, acc_ref = rest
    k = pl.program_id(2)
    part = jnp.dot(a_ref[...], w_ref[...], preferred_element_type=F32)

    @pl.when(k == 0)
    def _():
        acc_ref[...] = part

    @pl.when(k > 0)
    def _():
        acc_ref[...] += part

    @pl.when(k == nk - 1)
    def _():
        out = acc_ref[...]
        if residual:
            out = out + r_ref[...]
        o_ref[...] = out.astype(o_ref.dtype)


def _matmul(a, w, *, tm, tn, tk, out_dtype, residual=None, name):
    m, kd = a.shape
    n = w.shape[1]
    tm = min(tm, m)
    nk = kd // tk
    in_specs = [pl.BlockSpec((tm, tk), lambda i, j, k: (i, k)),
                pl.BlockSpec((tk, tn), lambda i, j, k: (k, j))]
    args = [a, w]
    if residual is not None:
        in_specs.append(pl.BlockSpec((tm, tn), lambda i, j, k: (i, j)))
        args.append(residual)
    return pl.pallas_call(
        functools.partial(_mm_kernel, nk=nk, residual=residual is not None),
        grid=(m // tm, n // tn, nk),
        in_specs=in_specs,
        out_specs=pl.BlockSpec((tm, tn), lambda i, j, k: (i, j)),
        out_shape=jax.ShapeDtypeStruct((m, n), out_dtype),
        scratch_shapes=[pltpu.VMEM((tm, tn), F32)],
        compiler_params=_params("parallel", "parallel", "arbitrary"),
        name=name,
    )(*args)


def _gate_up_kernel(a_ref, wg_ref, wu_ref, o_ref):
    a = a_ref[...]
    g = jnp.dot(a, wg_ref[...], preferred_element_type=F32)
    u = jnp.dot(a, wu_ref[...], preferred_element_type=F32)
    o_ref[...] = (g * jax.nn.sigmoid(g) * u).astype(o_ref.dtype)


def _gate_up(a, wg, wu, *, tm, tn):
    m, kd = a.shape
    n = wg.shape[1]
    tm = min(tm, m)
    return pl.pallas_call(
        _gate_up_kernel,
        grid=(m // tm, n // tn),
        in_specs=[pl.BlockSpec((tm, kd), lambda i, j: (i, 0)),
                  pl.BlockSpec((kd, tn), lambda i, j: (0, j)),
                  pl.BlockSpec((kd, tn), lambda i, j: (0, j))],
        out_specs=pl.BlockSpec((tm, tn), lambda i, j: (i, j)),
        out_shape=jax.ShapeDtypeStruct((m, n), BF16),
        compiler_params=_params("parallel", "parallel"),
        name="gate_up",
    )(a, wg, wu)


def _bdot(a, b):
    return jnp.dot(a.astype(BF16), b.astype(BF16), preferred_element_type=F32)


def _bdot_nt(a, b):
    return lax.dot_general(a.astype(BF16), b.astype(BF16), (((1,), (1,)), ((), ())),
                           preferred_element_type=F32)


def _bdot_tn(a, b):
    return lax.dot_general(a.astype(BF16), b.astype(BF16), (((0,), (0,)), ((), ())),
                           preferred_element_type=F32)


def _xdot(a, b):
    return jnp.dot(a, b, preferred_element_type=F32, precision=lax.Precision.HIGHEST)


def _gdn_kernel(q_ref, k_ref, v_ref, z_ref, ab_ref, cw_ref, alog_ref, dtb_ref, normw_ref,
                o_ref, xbuf, ybuf, s_ref, *, tb):
    t = pl.program_id(1)
    nh = DN_HEADS
    c = nh * HEAD_DIM
    pad = 8

    @pl.when(t == 0)
    def _():
        xbuf[0:pad, :] = jnp.zeros((pad, 3 * c), F32)
        s_ref[...] = jnp.zeros_like(s_ref)

    xbuf[pad:pad + tb, 0:c] = q_ref[...]
    xbuf[pad:pad + tb, c:2 * c] = k_ref[...]
    xbuf[pad:pad + tb, 2 * c:3 * c] = v_ref[...]

    cw = cw_ref[...]
    acc = None
    for j in range(CONV_WIDTH):
        start = pad - (CONV_WIDTH - 1) + j
        term = xbuf[start:start + tb, :] * cw[j:j + 1, :]
        acc = term if acc is None else acc + term
    ybuf[...] = acc * jax.nn.sigmoid(acc)
    xbuf[0:pad, :] = xbuf[tb:tb + pad, :]

    ri = lax.broadcasted_iota(jnp.int32, (CHUNK, CHUNK), 0)
    ci = lax.broadcasted_iota(jnp.int32, (CHUNK, CHUNK), 1)
    incl = ri >= ci
    strict = ri > ci
    eye_f = jnp.where(ri == ci, 1.0, 0.0).astype(F32)
    tri_ones = jnp.where(incl, 1.0, 0.0).astype(F32)
    level_masks = []
    for lvl in range(CHUNK.bit_length() - 1):
        same_pair = (ri >> (lvl + 1)) == (ci >> (lvl + 1))
        level_masks.append(same_pair & (((ri >> lvl) & 1) == 1) & (((ci >> lvl) & 1) == 0))
    alog = alog_ref[...]
    dtb = dtb_ref[...]
    normw = normw_ref[...]
    scale = HEAD_DIM ** -0.5
    heads = range(nh)

    def chunk_body(ch, carry):
        r0 = pl.multiple_of(ch * CHUNK, CHUNK)
        rows = pl.ds(r0, CHUNK)
        ab = ab_ref[rows, :]
        g_all = -jnp.exp(alog) * jax.nn.softplus(ab + dtb)
        beta_all = jax.nn.sigmoid(ab)
        gc_all = _xdot(tri_ones, g_all)
        eg_all = jnp.exp(gc_all)
        gc_last = gc_all[CHUNK - 1:CHUNK, :]
        tail_all = jnp.exp(gc_last - gc_all)
        gl_all = jnp.exp(gc_last)
        gc_t = jnp.concatenate([gc_all, jnp.zeros_like(gc_all)], axis=0).T

        def lanecol(x, h):
            return x[:, h:h + 1]

        def sl(section, h):
            lo = section * c + h * HEAD_DIM
            return ybuf[rows, lo:lo + HEAD_DIM]

        qn, kn, decay, beta = [], [], [], []
        for h in heads:
            qh = sl(0, h)
            kh = sl(1, h)
            qn.append(qh * lax.rsqrt(jnp.sum(qh * qh, axis=-1, keepdims=True) + NORM_EPS) * scale)
            kn.append(kh * lax.rsqrt(jnp.sum(kh * kh, axis=-1, keepdims=True) + NORM_EPS))
            gc_row = gc_t[h:h + 1, 0:CHUNK]
            decay.append(jnp.exp(jnp.where(incl, lanecol(gc_all, h) - gc_row, -jnp.inf)))
            beta.append(lanecol(beta_all, DN_HEADS + h))

        sc = [_bdot_nt(jnp.concatenate([qn[h], kn[h]], axis=0), kn[h]) for h in heads]
        qk = [sc[h][0:CHUNK] * decay[h] for h in heads]
        a_mat = [jnp.where(strict, sc[h][CHUNK:2 * CHUNK] * beta[h] * decay[h], 0.0) for h in heads]

        x_inv = [eye_f - jnp.where(level_masks[0], a_mat[h], 0.0) for h in heads]
        for mask in level_masks[1:]:
            y = [_bdot(jnp.where(mask, a_mat[h], 0.0), x_inv[h]) for h in heads]
            xy = [_bdot(x_inv[h], y[h]) for h in heads]
            x_inv = [x_inv[h] - xy[h] for h in heads]

        uw = []
        for h in heads:
            kb_eg = kn[h] * (beta[h] * lanecol(eg_all, h))
            uw.append(_bdot(x_inv[h], jnp.concatenate([sl(2, h) * beta[h], kb_eg], axis=1)))

        ws = []
        for h in heads:
            qd = qn[h] * lanecol(eg_all, h)
            ws.append(_bdot(jnp.concatenate([uw[h][:, HEAD_DIM:2 * HEAD_DIM], qd], axis=0), s_ref[h]))
        v_new = [uw[h][:, 0:HEAD_DIM] - ws[h][0:CHUNK] for h in heads]
        o_intra = [_bdot(qk[h], v_new[h]) for h in heads]
        s_add = [_bdot_tn(kn[h] * lanecol(tail_all, h), v_new[h]) for h in heads]
        for h in heads:
            s_ref[h] = s_ref[h] * lanecol(gl_all, h) + s_add[h]
            o = ws[h][CHUNK:2 * CHUNK] + o_intra[h]
            zh = z_ref[rows, h * HEAD_DIM:(h + 1) * HEAD_DIM]
            o = o * lax.rsqrt(jnp.mean(o * o, axis=-1, keepdims=True) + NORM_EPS) * normw
            o_ref[rows, h * HEAD_DIM:(h + 1) * HEAD_DIM] = (o * (zh * jax.nn.sigmoid(zh))).astype(o_ref.dtype)
        return carry

    lax.fori_loop(0, tb // CHUNK, chunk_body, 0)


def _gdn(proj, ab, conv_w, alog_lane, dtb_lane, norm_w, *, batch, seq, tb):
    m = proj.shape[0]
    c = DN_HEADS * HEAD_DIM
    nt = seq // tb

    def col(section):
        return lambda b, t: (b * nt + t, section)

    lane_spec = pl.BlockSpec((1, LANES), lambda b, t: (0, 0))
    return pl.pallas_call(
        functools.partial(_gdn_kernel, tb=tb),
        grid=(batch, nt),
        in_specs=[pl.BlockSpec((tb, c), col(0)), pl.BlockSpec((tb, c), col(1)),
                  pl.BlockSpec((tb, c), col(2)), pl.BlockSpec((tb, c), col(3)),
                  pl.BlockSpec((tb, LANES), lambda b, t: (b * nt + t, 0)),
                  pl.BlockSpec((CONV_WIDTH, 3 * c), lambda b, t: (0, 0)),
                  lane_spec, lane_spec, lane_spec],
        out_specs=pl.BlockSpec((tb, c), lambda b, t: (b * nt + t, 0)),
        out_shape=jax.ShapeDtypeStruct((m, c), BF16),
        scratch_shapes=[pltpu.VMEM((tb + 8, 3 * c), F32), pltpu.VMEM((tb, 3 * c), F32),
                        pltpu.VMEM((DN_HEADS, HEAD_DIM, HEAD_DIM), F32)],
        compiler_params=_params("parallel", "arbitrary"),
        name="gated_delta_rule",
    )(proj, proj, proj, proj, ab, conv_w, alog_lane, dtb_lane, norm_w)


def _swa_kernel(sinks_ref, q_ref, kp_ref, kc_ref, vp_ref, vc_ref, qw_ref, kw_ref, o_ref):
    kvh = pl.program_id(1)
    n = pl.program_id(2)
    blk = WINDOW
    rows = SWA_GROUP * blk

    def rms(x, w):
        return x * lax.rsqrt(jnp.mean(x * x, axis=-1, keepdims=True) + NORM_EPS) * w

    qw = qw_ref[...]
    kw = kw_ref[...]
    q = jnp.concatenate([rms(q_ref[:, g * HEAD_DIM:(g + 1) * HEAD_DIM], qw) for g in range(SWA_GROUP)], axis=0)
    k = jnp.concatenate([rms(kp_ref[...], kw), rms(kc_ref[...], kw)], axis=0)
    v = jnp.concatenate([vp_ref[...], vc_ref[...]], axis=0)

    s = _bdot_nt(q, k) * (HEAD_DIM ** -0.5)

    ri = lax.broadcasted_iota(jnp.int32, (rows, 2 * blk), 0)
    ki = lax.broadcasted_iota(jnp.int32, (rows, 2 * blk), 1)
    dist = (ri & (blk - 1)) + blk - ki
    valid = (dist >= 0) & (dist < WINDOW) & ((ki >= blk) | (n > 0))

    rcol = lax.broadcasted_iota(jnp.int32, (rows, 1), 0) >> (blk.bit_length() - 1)
    head = kvh * SWA_GROUP + rcol
    slope = jnp.exp2(-8.0 * (head + 1).astype(F32) / SWA_Q_HEADS)
    sink = jnp.zeros((rows, 1), F32)
    for g in range(SWA_GROUP):
        sink = jnp.where(rcol == g, sinks_ref[kvh * SWA_GROUP + g], sink)

    s = jnp.where(valid, s - slope * dist.astype(F32), -jnp.inf)
    mx = jnp.maximum(jnp.max(s, axis=-1, keepdims=True), sink)
    p = jnp.exp(s - mx)
    denom = jnp.sum(p, axis=-1, keepdims=True) + jnp.exp(sink - mx)
    o = _bdot(p, v) / denom
    for g in range(SWA_GROUP):
        o_ref[:, g * HEAD_DIM:(g + 1) * HEAD_DIM] = o[g * blk:(g + 1) * blk].astype(o_ref.dtype)


def _swa(proj, sinks, q_norm_w, k_norm_w, *, batch, seq, q_col, k_col, v_col):
    m = proj.shape[0]
    blk = WINDOW
    nb = seq // blk
    gw = SWA_GROUP * HEAD_DIM

    def cur(col):
        return lambda b, h, n: (b * nb + n, col + h)

    def prev(col):
        return lambda b, h, n: (b * nb + jnp.maximum(n - 1, 0), col + h)

    wspec = pl.BlockSpec((1, HEAD_DIM), lambda b, h, n: (0, 0))
    return pl.pallas_call(
        _swa_kernel,
        grid=(batch, SWA_KV_HEADS, nb),
        in_specs=[pl.BlockSpec(memory_space=pltpu.SMEM),
                  pl.BlockSpec((blk, gw), lambda b, h, n: (b * nb + n, q_col // SWA_GROUP + h)),
                  pl.BlockSpec((blk, HEAD_DIM), prev(k_col)), pl.BlockSpec((blk, HEAD_DIM), cur(k_col)),
                  pl.BlockSpec((blk, HEAD_DIM), prev(v_col)), pl.BlockSpec((blk, HEAD_DIM), cur(v_col)),
                  wspec, wspec],
        out_specs=pl.BlockSpec((blk, gw), lambda b, h, n: (b * nb + n, h)),
        out_shape=jax.ShapeDtypeStruct((m, SWA_Q_HEADS * HEAD_DIM), BF16),
        compiler_params=_params("parallel", "parallel", "arbitrary"),
        name="sliding_window_gqa",
    )(sinks, proj, proj, proj, proj, proj, q_norm_w, k_norm_w)


def _pad_lanes(v):
    return jnp.pad(v.astype(F32), (0, LANES - v.shape[0])).reshape(1, LANES)


def _layer(x2, batch, seq, attn_norm_w, w_in, conv_w, a_log, dt_bias, dn_norm_w, q_norm_w, k_norm_w,
           sinks, w_out, ffn_norm_w, w_gate, w_up, w_down):
    m, d = x2.shape
    dn_w = DN_HEADS * HEAD_DIM
    q_w = SWA_Q_HEADS * HEAD_DIM
    kv_w = SWA_KV_HEADS * HEAD_DIM
    gates_at = 4 * dn_w
    n_gates = 2 * DN_HEADS

    w_main = jnp.concatenate([w_in[:, :gates_at], w_in[:, gates_at + n_gates:]], axis=1).astype(BF16)
    w_ab = jnp.pad(w_in[:, gates_at:gates_at + n_gates], ((0, 0), (0, LANES - n_gates))).astype(BF16)
    f = w_gate.shape[1]
    f_pad = -(-f // 512) * 512
    wg = jnp.pad(w_gate, ((0, 0), (0, f_pad - f))).astype(BF16)
    wu = jnp.pad(w_up, ((0, 0), (0, f_pad - f))).astype(BF16)
    wd = jnp.pad(w_down, ((0, f_pad - f), (0, 0))).astype(BF16)
    wo = w_out.astype(BF16)

    h = _rmsnorm(x2, attn_norm_w)
    proj = _matmul(h, w_main, tm=1024, tn=512, tk=d, out_dtype=F32, name="in_proj")
    ab = _matmul(h, w_ab, tm=1024, tn=LANES, tk=d, out_dtype=F32, name="gate_proj")

    o_dn = _gdn(proj, ab, conv_w, _pad_lanes(a_log), _pad_lanes(dt_bias), dn_norm_w.reshape(1, HEAD_DIM),
                batch=batch, seq=seq, tb=min(256, seq))
    sec = 4 * dn_w // HEAD_DIM
    o_sw = _swa(proj, sinks.astype(F32), q_norm_w.reshape(1, HEAD_DIM), k_norm_w.reshape(1, HEAD_DIM),
                batch=batch, seq=seq, q_col=sec, k_col=sec + q_w // HEAD_DIM,
                v_col=sec + (q_w + kv_w) // HEAD_DIM)

    mixed = jnp.concatenate([o_dn, o_sw], axis=1)
    x1 = _matmul(mixed, wo, tm=1024, tn=1024, tk=1024, out_dtype=F32, residual=x2, name="out_proj")

    h2 = _rmsnorm(x1, ffn_norm_w)
    act = _gate_up(h2, wg, wu, tm=1024, tn=512)
    return _matmul(act, wd, tm=1024, tn=1024, tk=f_pad // 4, out_dtype=F32, residual=x1, name="down_proj")


def kernel(x, attn_norm_w, w_in, conv_w, a_log, dt_bias, dn_norm_w, q_norm_w, k_norm_w, sinks, w_out,
           ffn_norm_w, w_gate, w_up, w_down):
    batch, seq, d = x.shape
    x2 = x.reshape(batch * seq, d)
    for l in range(attn_norm_w.shape[0]):
        x2 = _layer(x2, batch, seq, attn_norm_w[l], w_in[l], conv_w[l], a_log[l], dt_bias[l], dn_norm_w[l],
                    q_norm_w[l], k_norm_w[l], sinks[l], w_out[l], ffn_norm_w[l], w_gate[l], w_up[l], w_down[l])
    return x2.reshape(batch, seq, d)
```

```python
import functools

import jax
import jax.numpy as jnp
from jax import lax
from jax.experimental import pallas as pl
from jax.experimental.pallas import tpu as pltpu

NORM_EPS = 1e-6
HEAD_DIM = 128
DN_HEADS = 16
CHUNK = 64
CONV_WIDTH = 4
SWA_Q_HEADS = 16
SWA_KV_HEADS = 4
SWA_GROUP = SWA_Q_HEADS // SWA_KV_HEADS
WINDOW = 128
LANES = 128
VMEM_LIMIT_BYTES = 56 * 1024 * 1024

BF16 = jnp.bfloat16
F32 = jnp.float32


def _params(*semantics):
    return pltpu.CompilerParams(dimension_semantics=semantics, vmem_limit_bytes=VMEM_LIMIT_BYTES)


def _rmsnorm_kernel(x_ref, w_ref, o_ref):
    x = x_ref[...]
    ms = jnp.mean(x * x, axis=-1, keepdims=True)
    o_ref[...] = (x * lax.rsqrt(ms + NORM_EPS) * w_ref[...]).astype(o_ref.dtype)


def _rmsnorm(x, w, tm=256):
    m, d = x.shape
    tm = min(tm, m)
    return pl.pallas_call(
        _rmsnorm_kernel,
        grid=(m // tm,),
        in_specs=[pl.BlockSpec((tm, d), lambda i: (i, 0)), pl.BlockSpec((1, d), lambda i: (0, 0))],
        out_specs=pl.BlockSpec((tm, d), lambda i: (i, 0)),
        out_shape=jax.ShapeDtypeStruct((m, d), BF16),
        compiler_params=_params("parallel"),
        name="rmsnorm",
    )(x, w.reshape(1, d))


def _mm_kernel(a_ref, w_ref, *rest, nk, residual):
    if residual:
        r_ref, o_ref, acc_ref = rest
    else:
        o_ref, acc_ref = rest
    k = pl.program_id(2)
    part = jnp.dot(a_ref[...], w_ref[...], preferred_element_type=F32)

    @pl.when(k == 0)
    def _():
        acc_ref[...] = part

    @pl.when(k > 0)
    def _():
        acc_ref[...] += part

    @pl.when(k == nk - 1)
    def _():
        out = acc_ref[...]
        if residual:
            out = out + r_ref[...]
        o_ref[...] = out.astype(o_ref.dtype)


def _matmul(a, w, *, tm, tn, tk, out_dtype, residual=None, name):
    m, kd = a.shape
    n = w.shape[1]
    tm = min(tm, m)
    nk = kd // tk
    in_specs = [pl.BlockSpec((tm, tk), lambda i, j, k: (i, k)),
                pl.BlockSpec((tk, tn), lambda i, j, k: (k, j))]
    args = [a, w]
    if residual is not None:
        in_specs.append(pl.BlockSpec((tm, tn), lambda i, j, k: (i, j)))
        args.append(residual)
    return pl.pallas_call(
        functools.partial(_mm_kernel, nk=nk, residual=residual is not None),
        grid=(m // tm, n // tn, nk),
        in_specs=in_specs,
        out_specs=pl.BlockSpec((tm, tn), lambda i, j, k: (i, j)),
        out_shape=jax.ShapeDtypeStruct((m, n), out_dtype),
        scratch_shapes=[pltpu.VMEM((tm, tn), F32)],
        compiler_params=_params("parallel", "parallel", "arbitrary"),
        name=name,
    )(*args)


def _mm_pair_kernel(a1_ref, a2_ref, w_ref, r_ref, o_ref):
    k1 = a1_ref.shape[1]
    acc = jnp.dot(a1_ref[...], w_ref[0:k1, :], preferred_element_type=F32)
    acc = acc + jnp.dot(a2_ref[...], w_ref[k1:, :], preferred_element_type=F32)
    o_ref[...] = (acc + r_ref[...]).astype(o_ref.dtype)


def _matmul_pair(a1, a2, w, residual, *, tm, tn, name):
    m, k1 = a1.shape
    k2 = a2.shape[1]
    n = w.shape[1]
    tm = min(tm, m)
    return pl.pallas_call(
        _mm_pair_kernel,
        grid=(m // tm, n // tn),
        in_specs=[pl.BlockSpec((tm, k1), lambda i, j: (i, 0)),
                  pl.BlockSpec((tm, k2), lambda i, j: (i, 0)),
                  pl.BlockSpec((k1 + k2, tn), lambda i, j: (0, j)),
                  pl.BlockSpec((tm, tn), lambda i, j: (i, j))],
        out_specs=pl.BlockSpec((tm, tn), lambda i, j: (i, j)),
        out_shape=jax.ShapeDtypeStruct((m, n), F32),
        compiler_params=_params("parallel", "parallel"),
        name=name,
    )(a1, a2, w, residual)


def _gate_up_kernel(a_ref, wg_ref, wu_ref, o_ref):
    a = a_ref[...]
    g = jnp.dot(a, wg_ref[...], preferred_element_type=F32)
    u = jnp.dot(a, wu_ref[...], preferred_element_type=F32)
    o_ref[...] = (g * jax.nn.sigmoid(g) * u).astype(o_ref.dtype)


def _gate_up(a, wg, wu, *, tm, tn):
    m, kd = a.shape
    n = wg.shape[1]
    tm = min(tm, m)
    return pl.pallas_call(
        _gate_up_kernel,
        grid=(m // tm, n // tn),
        in_specs=[pl.BlockSpec((tm, kd), lambda i, j: (i, 0)),
                  pl.BlockSpec((kd, tn), lambda i, j: (0, j)),
                  pl.BlockSpec((kd, tn), lambda i, j: (0, j))],
        out_specs=pl.BlockSpec((tm, tn), lambda i, j: (i, j)),
        out_shape=jax.ShapeDtypeStruct((m, n), BF16),
        compiler_params=_params("parallel", "parallel"),
        name="gate_up",
    )(a, wg, wu)


def _bdot(a, b):
    return jnp.dot(a.astype(BF16), b.astype(BF16), preferred_element_type=F32)


def _bdot_nt(a, b):
    return lax.dot_general(a.astype(BF16), b.astype(BF16), (((1,), (1,)), ((), ())),
                           preferred_element_type=F32)


def _bdot_tn(a, b):
    return lax.dot_general(a.astype(BF16), b.astype(BF16), (((0,), (0,)), ((), ())),
                           preferred_element_type=F32)


def _xdot(a, b):
    return jnp.dot(a, b, preferred_element_type=F32, precision=lax.Precision.HIGHEST)


def _gdn_kernel(q_ref, k_ref, v_ref, z_ref, ab_ref, cw_ref, alog_ref, dtb_ref, normw_ref,
                o_ref, tail_ref, s_ref, *, tb):
    t = pl.program_id(1)
    nh = DN_HEADS
    c = nh * HEAD_DIM
    pad = 8

    @pl.when(t == 0)
    def _():
        tail_ref[...] = jnp.zeros_like(tail_ref)
        s_ref[...] = jnp.zeros_like(s_ref)

    sections = (q_ref, k_ref, v_ref)

    ri = lax.broadcasted_iota(jnp.int32, (CHUNK, LANES), 0)
    lane = lax.broadcasted_iota(jnp.int32, (CHUNK, LANES), 1)
    ci = lane & (CHUNK - 1)
    first = lane < CHUNK
    first2 = lax.broadcasted_iota(jnp.int32, (2 * CHUNK, LANES), 1) < CHUNK
    top2 = lax.broadcasted_iota(jnp.int32, (2 * CHUNK, LANES), 0) < CHUNK
    incl = ri >= ci
    strict = ri > ci
    eye_f = jnp.where(ri == ci, 1.0, 0.0).astype(F32)
    ri_sq = lax.broadcasted_iota(jnp.int32, (CHUNK, CHUNK), 0)
    ci_sq = lax.broadcasted_iota(jnp.int32, (CHUNK, CHUNK), 1)
    tri_ones = jnp.where(ri_sq >= ci_sq, 1.0, 0.0).astype(F32)
    level_masks = []
    for lvl in range(CHUNK.bit_length() - 1):
        same_pair = (ri >> (lvl + 1)) == (ci >> (lvl + 1))
        level_masks.append(same_pair & (((ri >> lvl) & 1) == 1) & (((ci >> lvl) & 1) == 0))
    alog = alog_ref[...]
    dtb = dtb_ref[...]
    normw = normw_ref[...]
    scale = HEAD_DIM ** -0.5
    heads = range(nh)
    pairs = range(nh // 2)
    zeros_hd = jnp.zeros((CHUNK, HEAD_DIM), F32)

    def block_diag(m):
        mb = jnp.concatenate([m, m], axis=0).astype(BF16)
        return jnp.where(first2 == top2, mb, jnp.zeros_like(mb))

    def lanecol(x, h):
        return x[:, h:h + 1]

    ones_hd = jnp.ones((HEAD_DIM, HEAD_DIM), BF16)

    def lane_sum(x):
        return jnp.dot(x.astype(BF16), ones_hd, preferred_element_type=F32)

    group = 2
    members = range(group)

    def group_body(it, carry):
        base = it * (group * CHUNK)
        rows = [pl.ds(pl.multiple_of(base + j * CHUNK, CHUNK), CHUNK) for j in members]

        conv = {}

        def sl(j, section, h):
            if (0, section, h) not in conv:
                lo = section * c + h * HEAD_DIM
                w = cw_ref[:, lo:lo + HEAD_DIM]
                prev = tail_ref[:, lo:lo + HEAD_DIM]
                for m in members:
                    x = sections[section][rows[m], h * HEAD_DIM:(h + 1) * HEAD_DIM]
                    ext = jnp.concatenate([prev, x], axis=0)
                    acc = x * w[CONV_WIDTH - 1:CONV_WIDTH, :]
                    for shift in range(1, CONV_WIDTH):
                        tap = CONV_WIDTH - 1 - shift
                        acc = acc + pltpu.roll(ext, shift, axis=0)[pad:pad + CHUNK, :] * w[tap:tap + 1, :]
                    conv[m, section, h] = acc * jax.nn.sigmoid(acc)
                    prev = x[CHUNK - pad:CHUNK, :]
                tail_ref[:, lo:lo + HEAD_DIM] = prev
            return conv[j, section, h]

        beta_all, gc_all, eg_all, tail_all, gl_all, gc_t = [], [], [], [], [], []
        for j in members:
            ab = ab_ref[rows[j], :]
            g = -jnp.exp(alog) * jax.nn.softplus(ab + dtb)
            beta_all.append(jax.nn.sigmoid(ab))
            gc = _xdot(tri_ones, g)
            gc_last = gc[CHUNK - 1:CHUNK, :]
            gc_all.append(gc)
            eg_all.append(jnp.exp(gc))
            tail_all.append(jnp.exp(gc_last - gc))
            gl_all.append(jnp.exp(gc_last))
            gc_t.append(jnp.concatenate([gc, pltpu.roll(gc, LANES - 1, axis=1)], axis=0).T)

        qn = [[None] * nh for _ in members]
        kn = [[None] * nh for _ in members]
        beta = [[None] * nh for _ in members]
        for j in members:
            for h in heads:
                qh = sl(j, 0, h)
                kh = sl(j, 1, h)
                qn[j][h] = qh * lax.rsqrt(lane_sum(qh * qh) + NORM_EPS) * scale
                kn[j][h] = kh * lax.rsqrt(lane_sum(kh * kh) + NORM_EPS)
                beta[j][h] = lanecol(beta_all[j], DN_HEADS + h)

        units = [(j, p) for j in members for p in pairs]
        qk, a_mat = {}, {}
        for j, p in units:
            h0, h1 = 2 * p, 2 * p + 1
            gc_col = jnp.where(first, lanecol(gc_all[j], h0), lanecol(gc_all[j], h1))
            decay = jnp.exp(jnp.where(incl, gc_col - gc_t[j][h0:h0 + 1, :], -jnp.inf))
            beta_p = jnp.where(first, beta[j][h0], beta[j][h1])
            lhs = jnp.concatenate([jnp.concatenate([qn[j][h0], qn[j][h1]], axis=1),
                                   jnp.concatenate([kn[j][h0], kn[j][h1]], axis=1)], axis=0)
            rhs = jnp.concatenate([jnp.concatenate([kn[j][h0], zeros_hd], axis=1),
                                   jnp.concatenate([zeros_hd, kn[j][h1]], axis=1)], axis=0)
            sc = _bdot_nt(lhs, rhs)
            qk[j, p] = sc[0:CHUNK] * decay
            a_mat[j, p] = jnp.where(strict, sc[CHUNK:2 * CHUNK] * beta_p * decay, 0.0)

        x_inv = {u: eye_f - jnp.where(level_masks[0], a_mat[u], 0.0) for u in units}
        for mask in level_masks[1:]:
            y = {u: jnp.dot(jnp.where(mask, a_mat[u], 0.0).astype(BF16), block_diag(x_inv[u]),
                            preferred_element_type=F32) for u in units}
            xy = {u: jnp.dot(x_inv[u].astype(BF16), block_diag(y[u]), preferred_element_type=F32) for u in units}
            x_inv = {u: x_inv[u] - xy[u] for u in units}

        uw = [[None] * nh for _ in members]
        for j, p in units:
            rhs = []
            for h in (2 * p, 2 * p + 1):
                kb_eg = kn[j][h] * (beta[j][h] * lanecol(eg_all[j], h))
                rhs.append(jnp.concatenate([sl(j, 2, h) * beta[j][h], kb_eg], axis=1))
            both = jnp.dot(block_diag(x_inv[j, p]), jnp.concatenate(rhs, axis=0).astype(BF16),
                           preferred_element_type=F32)
            uw[j][2 * p] = both[0:CHUNK]
            uw[j][2 * p + 1] = both[CHUNK:2 * CHUNK]

        for j in members:
            ws = []
            for h in heads:
                qd = qn[j][h] * lanecol(eg_all[j], h)
                ws.append(_bdot(jnp.concatenate([uw[j][h][:, HEAD_DIM:2 * HEAD_DIM], qd], axis=0), s_ref[h]))
            v_new = [uw[j][h][:, 0:HEAD_DIM] - ws[h][0:CHUNK] for h in heads]
            o_intra = [None] * nh
            for p in pairs:
                both = jnp.dot(block_diag(qk[j, p]),
                               jnp.concatenate([v_new[2 * p], v_new[2 * p + 1]], axis=0).astype(BF16),
                               preferred_element_type=F32)
                o_intra[2 * p] = both[0:CHUNK]
                o_intra[2 * p + 1] = both[CHUNK:2 * CHUNK]
            s_add = [_bdot_tn(kn[j][h] * lanecol(tail_all[j], h), v_new[h]) for h in heads]
            for h in heads:
                s_ref[h] = s_ref[h] * lanecol(gl_all[j], h) + s_add[h]
                o = ws[h][CHUNK:2 * CHUNK] + o_intra[h]
                zh = z_ref[rows[j], h * HEAD_DIM:(h + 1) * HEAD_DIM]
                o = o * lax.rsqrt(jnp.mean(o * o, axis=-1, keepdims=True) + NORM_EPS) * normw
                o_ref[rows[j], h * HEAD_DIM:(h + 1) * HEAD_DIM] = (
                    o * (zh * jax.nn.sigmoid(zh))).astype(o_ref.dtype)
        return carry

    lax.fori_loop(0, tb // (group * CHUNK), group_body, 0)


def _gdn(proj, ab, conv_w, alog_lane, dtb_lane, norm_w, *, batch, seq, tb):
    m = proj.shape[0]
    c = DN_HEADS * HEAD_DIM
    nt = seq // tb

    def col(section):
        return lambda b, t: (b * nt + t, section)

    lane_spec = pl.BlockSpec((1, LANES), lambda b, t: (0, 0))
    return pl.pallas_call(
        functools.partial(_gdn_kernel, tb=tb),
        grid=(batch, nt),
        in_specs=[pl.BlockSpec((tb, c), col(0)), pl.BlockSpec((tb, c), col(1)),
                  pl.BlockSpec((tb, c), col(2)), pl.BlockSpec((tb, c), col(3)),
                  pl.BlockSpec((tb, LANES), lambda b, t: (b * nt + t, 0)),
                  pl.BlockSpec((CONV_WIDTH, 3 * c), lambda b, t: (0, 0)),
                  lane_spec, lane_spec, lane_spec],
        out_specs=pl.BlockSpec((tb, c), lambda b, t: (b * nt + t, 0)),
        out_shape=jax.ShapeDtypeStruct((m, c), BF16),
        scratch_shapes=[pltpu.VMEM((8, 3 * c), F32), pltpu.VMEM((DN_HEADS, HEAD_DIM, HEAD_DIM), F32)],
        compiler_params=_params("parallel", "arbitrary"),
        name="gated_delta_rule",
    )(proj, proj, proj, proj, ab, conv_w, alog_lane, dtb_lane, norm_w)


def _swa_kernel(sinks_ref, q_ref, kp_ref, kc_ref, vp_ref, vc_ref, qw_ref, kw_ref, o_ref, bias_ref, sink_ref):
    kvh = pl.program_id(1)
    n = pl.program_id(2)
    blk = WINDOW
    rows = SWA_GROUP * blk

    def build_bias(first_block):
        ri = lax.broadcasted_iota(jnp.int32, (rows, 2 * blk), 0)
        ki = lax.broadcasted_iota(jnp.int32, (rows, 2 * blk), 1)
        dist = (ri & (blk - 1)) + blk - ki
        head = kvh * SWA_GROUP + (ri >> (blk.bit_length() - 1))
        slope = jnp.exp2(-8.0 * (head + 1).astype(F32) / SWA_Q_HEADS)
        valid = (dist >= 0) & (dist < WINDOW)
        if first_block:
            valid = valid & (ki >= blk)
        bias_ref[...] = jnp.where(valid, -slope * dist.astype(F32), -jnp.inf)

    @pl.when(n == 0)
    def _():
        build_bias(True)
        for g in range(SWA_GROUP):
            sink_ref[g * blk:(g + 1) * blk, :] = jnp.full((blk, LANES), sinks_ref[kvh * SWA_GROUP + g], F32)

    @pl.when(n == 1)
    def _():
        build_bias(False)

    def rms(x, w):
        return x * lax.rsqrt(jnp.mean(x * x, axis=-1, keepdims=True) + NORM_EPS) * w

    qw = qw_ref[...] * (HEAD_DIM ** -0.5)
    kw = kw_ref[...]
    q = jnp.concatenate([rms(q_ref[:, g * HEAD_DIM:(g + 1) * HEAD_DIM], qw) for g in range(SWA_GROUP)], axis=0)
    k = jnp.concatenate([rms(kp_ref[...], kw), rms(kc_ref[...], kw)], axis=0)
    v = jnp.concatenate([vp_ref[...], vc_ref[...]], axis=0)

    s = _bdot_nt(q, k) + bias_ref[...]
    sink = sink_ref[...]
    mx = jnp.maximum(jnp.max(s, axis=-1, keepdims=True), sink)
    p = jnp.exp(s - jnp.concatenate([mx, mx], axis=1))
    denom = jnp.sum(p, axis=-1, keepdims=True) + jnp.exp(sink - mx)
    o = _bdot(p, v) / denom
    for g in range(SWA_GROUP):
        o_ref[:, g * HEAD_DIM:(g + 1) * HEAD_DIM] = o[g * blk:(g + 1) * blk].astype(o_ref.dtype)


def _swa(proj, sinks, q_norm_w, k_norm_w, *, batch, seq, q_col, k_col, v_col):
    m = proj.shape[0]
    blk = WINDOW
    nb = seq // blk
    gw = SWA_GROUP * HEAD_DIM

    def cur(col):
        return lambda b, h, n: (b * nb + n, col + h)

    def prev(col):
        return lambda b, h, n: (b * nb + jnp.maximum(n - 1, 0), col + h)

    wspec = pl.BlockSpec((1, HEAD_DIM), lambda b, h, n: (0, 0))
    return pl.pallas_call(
        _swa_kernel,
        grid=(batch, SWA_KV_HEADS, nb),
        in_specs=[pl.BlockSpec(memory_space=pltpu.SMEM),
                  pl.BlockSpec((blk, gw), lambda b, h, n: (b * nb + n, q_col // SWA_GROUP + h)),
                  pl.BlockSpec((blk, HEAD_DIM), prev(k_col)), pl.BlockSpec((blk, HEAD_DIM), cur(k_col)),
                  pl.BlockSpec((blk, HEAD_DIM), prev(v_col)), pl.BlockSpec((blk, HEAD_DIM), cur(v_col)),
                  wspec, wspec],
        out_specs=pl.BlockSpec((blk, gw), lambda b, h, n: (b * nb + n, h)),
        out_shape=jax.ShapeDtypeStruct((m, SWA_Q_HEADS * HEAD_DIM), BF16),
        scratch_shapes=[pltpu.VMEM((SWA_GROUP * blk, 2 * blk), F32), pltpu.VMEM((SWA_GROUP * blk, LANES), F32)],
        compiler_params=_params("parallel", "parallel", "arbitrary"),
        name="sliding_window_gqa",
    )(sinks, proj, proj, proj, proj, proj, q_norm_w, k_norm_w)


def _pad_lanes(v):
    return jnp.pad(v.astype(F32), (0, LANES - v.shape[0])).reshape(1, LANES)


def _layer(x2, batch, seq, attn_norm_w, w_in, conv_w, a_log, dt_bias, dn_norm_w, q_norm_w, k_norm_w,
           sinks, w_out, ffn_norm_w, w_gate, w_up, w_down):
    m, d = x2.shape
    dn_w = DN_HEADS * HEAD_DIM
    q_w = SWA_Q_HEADS * HEAD_DIM
    kv_w = SWA_KV_HEADS * HEAD_DIM
    gates_at = 4 * dn_w
    n_gates = 2 * DN_HEADS

    w_in16 = w_in.astype(BF16)
    w_main = jnp.concatenate([w_in16[:, :gates_at], w_in16[:, gates_at + n_gates:]], axis=1)
    w_ab = jnp.pad(w_in16[:, gates_at:gates_at + n_gates], ((0, 0), (0, LANES - n_gates)))
    f = w_gate.shape[1]
    f_pad = -(-f // 512) * 512
    wg = jnp.pad(w_gate, ((0, 0), (0, f_pad - f))).astype(BF16)
    wu = jnp.pad(w_up, ((0, 0), (0, f_pad - f))).astype(BF16)
    wd = jnp.pad(w_down, ((0, f_pad - f), (0, 0))).astype(BF16)
    wo = w_out.astype(BF16)

    h = _rmsnorm(x2, attn_norm_w)
    proj = _matmul(h, w_main, tm=1024, tn=512, tk=d, out_dtype=F32, name="in_proj")
    ab = _matmul(h, w_ab, tm=1024, tn=LANES, tk=d, out_dtype=F32, name="gate_proj")

    o_dn = _gdn(proj, ab, conv_w, _pad_lanes(a_log), _pad_lanes(dt_bias), dn_norm_w.reshape(1, HEAD_DIM),
                batch=batch, seq=seq, tb=min(256, seq))
    sec = 4 * dn_w // HEAD_DIM
    o_sw = _swa(proj, sinks.astype(F32), q_norm_w.reshape(1, HEAD_DIM), k_norm_w.reshape(1, HEAD_DIM),
                batch=batch, seq=seq, q_col=sec, k_col=sec + q_w // HEAD_DIM,
                v_col=sec + (q_w + kv_w) // HEAD_DIM)

    x1 = _matmul_pair(o_dn, o_sw, wo, x2, tm=1024, tn=512, name="out_proj")

    h2 = _rmsnorm(x1, ffn_norm_w)
    act = _gate_up(h2, wg, wu, tm=1024, tn=512)
    return _matmul(act, wd, tm=1024, tn=1024, tk=f_pad // 4, out_dtype=F32, residual=x1, name="down_proj")


def kernel(x, attn_norm_w, w_in, conv_w, a_log, dt_bias, dn_norm_w, q_norm_w, k_norm_w, sinks, w_out,
           ffn_norm_w, w_gate, w_up, w_down):
    batch, seq, d = x.shape
    x2 = x.reshape(batch * seq, d)
    for l in range(attn_norm_w.shape[0]):
        x2 = _layer(x2, batch, seq, attn_norm_w[l], w_in[l], conv_w[l], a_log[l], dt_bias[l], dn_norm_w[l],
                    q_norm_w[l], k_norm_w[l], sinks[l], w_out[l], ffn_norm_w[l], w_gate[l], w_up[l], w_down[l])
    return x2.reshape(batch, seq, d)
```

```python
import functools

import jax
import jax.numpy as jnp
from jax import lax
from jax.experimental import pallas as pl
from jax.experimental.pallas import tpu as pltpu

NORM_EPS = 1e-6
HEAD_DIM = 128
DN_HEADS = 16
CHUNK = 64
CONV_WIDTH = 4
SWA_Q_HEADS = 16
SWA_KV_HEADS = 4
SWA_GROUP = SWA_Q_HEADS // SWA_KV_HEADS
WINDOW = 128
LANES = 128
VMEM_LIMIT_BYTES = 56 * 1024 * 1024

BF16 = jnp.bfloat16
F32 = jnp.float32


def _params(*semantics):
    return pltpu.CompilerParams(dimension_semantics=semantics, vmem_limit_bytes=VMEM_LIMIT_BYTES)


def _rmsnorm_kernel(x_ref, w_ref, o_ref):
    x = x_ref[...]
    ms = jnp.mean(x * x, axis=-1, keepdims=True)
    o_ref[...] = (x * lax.rsqrt(ms + NORM_EPS) * w_ref[...]).astype(o_ref.dtype)


def _rmsnorm(x, w, tm=256):
    m, d = x.shape
    tm = min(tm, m)
    return pl.pallas_call(
        _rmsnorm_kernel,
        grid=(m // tm,),
        in_specs=[pl.BlockSpec((tm, d), lambda i: (i, 0)), pl.BlockSpec((1, d), lambda i: (0, 0))],
        out_specs=pl.BlockSpec((tm, d), lambda i: (i, 0)),
        out_shape=jax.ShapeDtypeStruct((m, d), BF16),
        compiler_params=_params("parallel"),
        name="rmsnorm",
    )(x, w.reshape(1, d))


def _cast_cols_kernel(w_ref, o_ref):
    n_in = w_ref.shape[1]
    o_ref[:, 0:n_in] = w_ref[...].astype(o_ref.dtype)
    if o_ref.shape[1] > n_in:
        o_ref[:, n_in:] = jnp.zeros((o_ref.shape[0], o_ref.shape[1] - n_in), o_ref.dtype)


def _cast_pad_cols(w, n_out, *, tr, name):
    k, n = w.shape
    return pl.pallas_call(
        _cast_cols_kernel,
        grid=(k // tr,),
        in_specs=[pl.BlockSpec((tr, n), lambda i: (i, 0))],
        out_specs=pl.BlockSpec((tr, n_out), lambda i: (i, 0)),
        out_shape=jax.ShapeDtypeStruct((k, n_out), BF16),
        compiler_params=_params("parallel"),
        name=name,
    )(w)


def _cast_rows_kernel(w_ref, o_ref, *, n_valid):
    i = pl.program_id(0)

    @pl.when(i < n_valid)
    def _():
        o_ref[...] = w_ref[...].astype(o_ref.dtype)

    @pl.when(i >= n_valid)
    def _():
        o_ref[...] = jnp.zeros_like(o_ref)


def _cast_pad_rows(w, k_out, *, tr, name):
    k, n = w.shape
    n_valid = k // tr
    return pl.pallas_call(
        functools.partial(_cast_rows_kernel, n_valid=n_valid),
        grid=(k_out // tr,),
        in_specs=[pl.BlockSpec((tr, n), lambda i: (jnp.minimum(i, n_valid - 1), 0))],
        out_specs=pl.BlockSpec((tr, n), lambda i: (i, 0)),
        out_shape=jax.ShapeDtypeStruct((k_out, n), BF16),
        compiler_params=_params("parallel"),
        name=name,
    )(w)


def _split_w_in_kernel(w_ref, main_ref, gates_ref, *, gates_at, n_gates):
    n_main = main_ref.shape[1]
    main_ref[:, 0:gates_at] = w_ref[:, 0:gates_at].astype(BF16)
    main_ref[:, gates_at:] = w_ref[:, gates_at + n_gates:gates_at + n_gates + (n_main - gates_at)].astype(BF16)
    gates_ref[:, 0:n_gates] = w_ref[:, gates_at:gates_at + n_gates].astype(BF16)
    gates_ref[:, n_gates:] = jnp.zeros((gates_ref.shape[0], LANES - n_gates), BF16)


def _split_w_in(w_in, *, gates_at, n_gates, tr):
    k, n = w_in.shape
    return pl.pallas_call(
        functools.partial(_split_w_in_kernel, gates_at=gates_at, n_gates=n_gates),
        grid=(k // tr,),
        in_specs=[pl.BlockSpec((tr, n), lambda i: (i, 0))],
        out_specs=[pl.BlockSpec((tr, n - n_gates), lambda i: (i, 0)),
                   pl.BlockSpec((tr, LANES), lambda i: (i, 0))],
        out_shape=[jax.ShapeDtypeStruct((k, n - n_gates), BF16), jax.ShapeDtypeStruct((k, LANES), BF16)],
        compiler_params=_params("parallel"),
        name="split_w_in",
    )(w_in)


def _mm_kernel(a_ref, w_ref, *rest, nk, residual):
    r_ref = rest[0] if residual else None
    o_ref = rest[1 if residual else 0]
    part = jnp.dot(a_ref[...], w_ref[...], preferred_element_type=F32)
    if nk == 1:
        if residual:
            part = part + r_ref[...]
        o_ref[...] = part.astype(o_ref.dtype)
        return
    acc_ref = rest[-1]
    k = pl.program_id(2)

    @pl.when(k == 0)
    def _():
        acc_ref[...] = part

    @pl.when(k > 0)
    def _():
        acc_ref[...] += part

    @pl.when(k == nk - 1)
    def _():
        out = acc_ref[...]
        if residual:
            out = out + r_ref[...]
        o_ref[...] = out.astype(o_ref.dtype)


def _matmul(a, w, *, tm, tn, tk, out_dtype, residual=None, name):
    m, kd = a.shape
    n = w.shape[1]
    tm = min(tm, m)
    nk = kd // tk
    in_specs = [pl.BlockSpec((tm, tk), lambda i, j, k: (i, k)),
                pl.BlockSpec((tk, tn), lambda i, j, k: (k, j))]
    args = [a, w]
    if residual is not None:
        in_specs.append(pl.BlockSpec((tm, tn), lambda i, j, k: (i, j)))
        args.append(residual)
    return pl.pallas_call(
        functools.partial(_mm_kernel, nk=nk, residual=residual is not None),
        grid=(m // tm, n // tn, nk),
        in_specs=in_specs,
        out_specs=pl.BlockSpec((tm, tn), lambda i, j, k: (i, j)),
        out_shape=jax.ShapeDtypeStruct((m, n), out_dtype),
        scratch_shapes=[pltpu.VMEM((tm, tn), F32)] if nk > 1 else [],
        compiler_params=_params("parallel", "parallel", "arbitrary"),
        name=name,
    )(*args)


def _mm_pair_kernel(a1_ref, a2_ref, w_ref, r_ref, o_ref):
    k1 = a1_ref.shape[1]
    acc = jnp.dot(a1_ref[...], w_ref[0:k1, :], preferred_element_type=F32)
    acc = acc + jnp.dot(a2_ref[...], w_ref[k1:, :], preferred_element_type=F32)
    o_ref[...] = (acc + r_ref[...]).astype(o_ref.dtype)


def _matmul_pair(a1, a2, w, residual, *, tm, tn, name):
    m, k1 = a1.shape
    k2 = a2.shape[1]
    n = w.shape[1]
    tm = min(tm, m)
    return pl.pallas_call(
        _mm_pair_kernel,
        grid=(m // tm, n // tn),
        in_specs=[pl.BlockSpec((tm, k1), lambda i, j: (i, 0)),
                  pl.BlockSpec((tm, k2), lambda i, j: (i, 0)),
                  pl.BlockSpec((k1 + k2, tn), lambda i, j: (0, j)),
                  pl.BlockSpec((tm, tn), lambda i, j: (i, j))],
        out_specs=pl.BlockSpec((tm, tn), lambda i, j: (i, j)),
        out_shape=jax.ShapeDtypeStruct((m, n), F32),
        compiler_params=_params("parallel", "parallel"),
        name=name,
    )(a1, a2, w, residual)


def _gate_up_kernel(a_ref, wg_ref, wu_ref, o_ref):
    a = a_ref[...]
    g = jnp.dot(a, wg_ref[...], preferred_element_type=F32)
    u = jnp.dot(a, wu_ref[...], preferred_element_type=F32)
    o_ref[...] = (g * jax.nn.sigmoid(g) * u).astype(o_ref.dtype)


def _gate_up(a, wg, wu, *, tm, tn):
    m, kd = a.shape
    n = wg.shape[1]
    tm = min(tm, m)
    return pl.pallas_call(
        _gate_up_kernel,
        grid=(m // tm, n // tn),
        in_specs=[pl.BlockSpec((tm, kd), lambda i, j: (i, 0)),
                  pl.BlockSpec((kd, tn), lambda i, j: (0, j)),
                  pl.BlockSpec((kd, tn), lambda i, j: (0, j))],
        out_specs=pl.BlockSpec((tm, tn), lambda i, j: (i, j)),
        out_shape=jax.ShapeDtypeStruct((m, n), BF16),
        compiler_params=_params("parallel", "parallel"),
        name="gate_up",
    )(a, wg, wu)


def _bdot(a, b):
    return jnp.dot(a.astype(BF16), b.astype(BF16), preferred_element_type=F32)


def _bdot_nt(a, b):
    return lax.dot_general(a.astype(BF16), b.astype(BF16), (((1,), (1,)), ((), ())),
                           preferred_element_type=F32)


def _bdot_tn(a, b):
    return lax.dot_general(a.astype(BF16), b.astype(BF16), (((0,), (0,)), ((), ())),
                           preferred_element_type=F32)


def _xdot(a, b):
    return jnp.dot(a, b, preferred_element_type=F32, precision=lax.Precision.HIGHEST)


def _gdn_kernel(q_ref, k_ref, v_ref, z_ref, ab_ref, cw_ref, alog_ref, dtb_ref, normw_ref,
                o_ref, tail_ref, s_ref, *, tb):
    t = pl.program_id(1)
    nh = DN_HEADS
    c = nh * HEAD_DIM
    pad = 8

    @pl.when(t == 0)
    def _():
        tail_ref[...] = jnp.zeros_like(tail_ref)
        s_ref[...] = jnp.zeros_like(s_ref)

    sections = (q_ref, k_ref, v_ref)

    ri = lax.broadcasted_iota(jnp.int32, (CHUNK, LANES), 0)
    lane = lax.broadcasted_iota(jnp.int32, (CHUNK, LANES), 1)
    ci = lane & (CHUNK - 1)
    first = lane < CHUNK
    first2 = lax.broadcasted_iota(jnp.int32, (2 * CHUNK, LANES), 1) < CHUNK
    top2 = lax.broadcasted_iota(jnp.int32, (2 * CHUNK, LANES), 0) < CHUNK
    incl = ri >= ci
    strict = ri > ci
    eye_f = jnp.where(ri == ci, 1.0, 0.0).astype(F32)
    ri_sq = lax.broadcasted_iota(jnp.int32, (CHUNK, CHUNK), 0)
    ci_sq = lax.broadcasted_iota(jnp.int32, (CHUNK, CHUNK), 1)
    tri_ones = jnp.where(ri_sq >= ci_sq, 1.0, 0.0).astype(F32)
    level_masks = []
    for lvl in range(CHUNK.bit_length() - 1):
        same_pair = (ri >> (lvl + 1)) == (ci >> (lvl + 1))
        level_masks.append(same_pair & (((ri >> lvl) & 1) == 1) & (((ci >> lvl) & 1) == 0))
    alog = alog_ref[...]
    dtb = dtb_ref[...]
    normw = normw_ref[...]
    scale = HEAD_DIM ** -0.5
    heads = range(nh)
    pairs = range(nh // 2)
    zeros_hd = jnp.zeros((CHUNK, HEAD_DIM), F32)

    def block_diag(m):
        mb = jnp.concatenate([m, m], axis=0).astype(BF16)
        return jnp.where(first2 == top2, mb, jnp.zeros_like(mb))

    def lanecol(x, h):
        return x[:, h:h + 1]

    ones_hd = jnp.ones((HEAD_DIM, HEAD_DIM), BF16)

    def lane_sum(x):
        return jnp.dot(x.astype(BF16), ones_hd, preferred_element_type=F32)

    group = 2
    members = range(group)

    def group_body(it, carry):
        base = it * (group * CHUNK)
        rows = [pl.ds(pl.multiple_of(base + j * CHUNK, CHUNK), CHUNK) for j in members]

        conv = {}

        def sl(j, section, h):
            if (0, section, h) not in conv:
                lo = section * c + h * HEAD_DIM
                w = cw_ref[:, lo:lo + HEAD_DIM]
                prev = tail_ref[:, lo:lo + HEAD_DIM]
                for m in members:
                    x = sections[section][rows[m], h * HEAD_DIM:(h + 1) * HEAD_DIM]
                    ext = jnp.concatenate([prev, x], axis=0)
                    acc = x * w[CONV_WIDTH - 1:CONV_WIDTH, :]
                    for shift in range(1, CONV_WIDTH):
                        tap = CONV_WIDTH - 1 - shift
                        acc = acc + pltpu.roll(ext, shift, axis=0)[pad:pad + CHUNK, :] * w[tap:tap + 1, :]
                    conv[m, section, h] = acc * jax.nn.sigmoid(acc)
                    prev = x[CHUNK - pad:CHUNK, :]
                tail_ref[:, lo:lo + HEAD_DIM] = prev
            return conv[j, section, h]

        beta_all, gc_all, eg_all, tail_all, gl_all, gc_t = [], [], [], [], [], []
        for j in members:
            ab = ab_ref[rows[j], :]
            g = -jnp.exp(alog) * jax.nn.softplus(ab + dtb)
            beta_all.append(jax.nn.sigmoid(ab))
            gc = _xdot(tri_ones, g)
            gc_last = gc[CHUNK - 1:CHUNK, :]
            gc_all.append(gc)
            eg_all.append(jnp.exp(gc))
            tail_all.append(jnp.exp(gc_last - gc))
            gl_all.append(jnp.exp(gc_last))
            gc_t.append(jnp.concatenate([gc, pltpu.roll(gc, LANES - 1, axis=1)], axis=0).T)

        qn = [[None] * nh for _ in members]
        kn = [[None] * nh for _ in members]
        beta = [[None] * nh for _ in members]
        for j in members:
            for h in heads:
                qh = sl(j, 0, h)
                kh = sl(j, 1, h)
                qn[j][h] = qh * lax.rsqrt(lane_sum(qh * qh) + NORM_EPS) * scale
                kn[j][h] = kh * lax.rsqrt(lane_sum(kh * kh) + NORM_EPS)
                beta[j][h] = lanecol(beta_all[j], DN_HEADS + h)

        units = [(j, p) for j in members for p in pairs]
        qk, a_mat = {}, {}
        for j, p in units:
            h0, h1 = 2 * p, 2 * p + 1
            gc_col = jnp.where(first, lanecol(gc_all[j], h0), lanecol(gc_all[j], h1))
            decay = jnp.exp(jnp.where(incl, gc_col - gc_t[j][h0:h0 + 1, :], -jnp.inf))
            beta_p = jnp.where(first, beta[j][h0], beta[j][h1])
            lhs = jnp.concatenate([jnp.concatenate([qn[j][h0], qn[j][h1]], axis=1),
                                   jnp.concatenate([kn[j][h0], kn[j][h1]], axis=1)], axis=0)
            rhs = jnp.concatenate([jnp.concatenate([kn[j][h0], zeros_hd], axis=1),
                                   jnp.concatenate([zeros_hd, kn[j][h1]], axis=1)], axis=0)
            sc = _bdot_nt(lhs, rhs)
            qk[j, p] = sc[0:CHUNK] * decay
            a_mat[j, p] = jnp.where(strict, sc[CHUNK:2 * CHUNK] * beta_p * decay, 0.0)

        x_inv = {u: eye_f - jnp.where(level_masks[0], a_mat[u], 0.0) for u in units}
        for mask in level_masks[1:]:
            y = {u: jnp.dot(jnp.where(mask, a_mat[u], 0.0).astype(BF16), block_diag(x_inv[u]),
                            preferred_element_type=F32) for u in units}
            xy = {u: jnp.dot(x_inv[u].astype(BF16), block_diag(y[u]), preferred_element_type=F32) for u in units}
            x_inv = {u: x_inv[u] - xy[u] for u in units}

        uw = [[None] * nh for _ in members]
        for j, p in units:
            rhs = []
            for h in (2 * p, 2 * p + 1):
                kb_eg = kn[j][h] * (beta[j][h] * lanecol(eg_all[j], h))
                rhs.append(jnp.concatenate([sl(j, 2, h) * beta[j][h], kb_eg], axis=1))
            both = jnp.dot(block_diag(x_inv[j, p]), jnp.concatenate(rhs, axis=0).astype(BF16),
                           preferred_element_type=F32)
            uw[j][2 * p] = both[0:CHUNK]
            uw[j][2 * p + 1] = both[CHUNK:2 * CHUNK]

        for j in members:
            ws = []
            for h in heads:
                qd = qn[j][h] * lanecol(eg_all[j], h)
                ws.append(_bdot(jnp.concatenate([uw[j][h][:, HEAD_DIM:2 * HEAD_DIM], qd], axis=0), s_ref[h]))
            v_new = [uw[j][h][:, 0:HEAD_DIM] - ws[h][0:CHUNK] for h in heads]
            o_intra = [None] * nh
            for p in pairs:
                both = jnp.dot(block_diag(qk[j, p]),
                               jnp.concatenate([v_new[2 * p], v_new[2 * p + 1]], axis=0).astype(BF16),
                               preferred_element_type=F32)
                o_intra[2 * p] = both[0:CHUNK]
                o_intra[2 * p + 1] = both[CHUNK:2 * CHUNK]
            s_add = [_bdot_tn(kn[j][h] * lanecol(tail_all[j], h), v_new[h]) for h in heads]
            for h in heads:
                s_ref[h] = s_ref[h] * lanecol(gl_all[j], h) + s_add[h]
                o = ws[h][CHUNK:2 * CHUNK] + o_intra[h]
                zh = z_ref[rows[j], h * HEAD_DIM:(h + 1) * HEAD_DIM]
                o = o * lax.rsqrt(jnp.mean(o * o, axis=-1, keepdims=True) + NORM_EPS) * normw
                o_ref[rows[j], h * HEAD_DIM:(h + 1) * HEAD_DIM] = (
                    o * (zh * jax.nn.sigmoid(zh))).astype(o_ref.dtype)
        return carry

    lax.fori_loop(0, tb // (group * CHUNK), group_body, 0)


def _gdn(proj, ab, conv_w, alog_lane, dtb_lane, norm_w, *, batch, seq, tb):
    m = proj.shape[0]
    c = DN_HEADS * HEAD_DIM
    nt = seq // tb

    def col(section):
        return lambda b, t: (b * nt + t, section)

    lane_spec = pl.BlockSpec((1, LANES), lambda b, t: (0, 0))
    return pl.pallas_call(
        functools.partial(_gdn_kernel, tb=tb),
        grid=(batch, nt),
        in_specs=[pl.BlockSpec((tb, c), col(0)), pl.BlockSpec((tb, c), col(1)),
                  pl.BlockSpec((tb, c), col(2)), pl.BlockSpec((tb, c), col(3)),
                  pl.BlockSpec((tb, LANES), lambda b, t: (b * nt + t, 0)),
                  pl.BlockSpec((CONV_WIDTH, 3 * c), lambda b, t: (0, 0)),
                  lane_spec, lane_spec, lane_spec],
        out_specs=pl.BlockSpec((tb, c), lambda b, t: (b * nt + t, 0)),
        out_shape=jax.ShapeDtypeStruct((m, c), BF16),
        scratch_shapes=[pltpu.VMEM((8, 3 * c), F32), pltpu.VMEM((DN_HEADS, HEAD_DIM, HEAD_DIM), F32)],
        compiler_params=_params("parallel", "arbitrary"),
        name="gated_delta_rule",
    )(proj, proj, proj, proj, ab, conv_w, alog_lane, dtb_lane, norm_w)


def _swa_kernel(sinks_ref, q_ref, kp_ref, kc_ref, vp_ref, vc_ref, qw_ref, kw_ref, o_ref,
                bias_ref, bias_first_ref, sink_ref, *, nq):
    kvh = pl.program_id(1)
    i = pl.program_id(2)
    blk = WINDOW
    rows = SWA_GROUP * blk

    @pl.when(i == 0)
    def _():
        ri = lax.broadcasted_iota(jnp.int32, (rows, 2 * blk), 0)
        ki = lax.broadcasted_iota(jnp.int32, (rows, 2 * blk), 1)
        dist = (ri & (blk - 1)) + blk - ki
        head = kvh * SWA_GROUP + (ri >> (blk.bit_length() - 1))
        slope = jnp.exp2(-8.0 * (head + 1).astype(F32) / SWA_Q_HEADS)
        bias = jnp.where((dist >= 0) & (dist < WINDOW), -slope * dist.astype(F32), -jnp.inf)
        bias_ref[...] = bias
        bias_first_ref[...] = jnp.where(ki >= blk, bias, -jnp.inf)
        for g in range(SWA_GROUP):
            sink_ref[g * blk:(g + 1) * blk, :] = jnp.full((blk, LANES), sinks_ref[kvh * SWA_GROUP + g], F32)

    @pl.when(i == 1)
    def _():
        bias_first_ref[...] = bias_ref[...]

    def rms(x, w):
        return x * lax.rsqrt(jnp.mean(x * x, axis=-1, keepdims=True) + NORM_EPS) * w

    qw = qw_ref[...] * (HEAD_DIM ** -0.5)
    kw = kw_ref[...]
    subs = range(nq)
    kb = [rms(kp_ref[...], kw)] + [rms(kc_ref[s * blk:(s + 1) * blk, :], kw) for s in subs]
    vb = [vp_ref[...]] + [vc_ref[s * blk:(s + 1) * blk, :] for s in subs]
    q = [jnp.concatenate([rms(q_ref[s * blk:(s + 1) * blk, g * HEAD_DIM:(g + 1) * HEAD_DIM], qw)
                          for g in range(SWA_GROUP)], axis=0) for s in subs]
    sc = [_bdot_nt(q[s], jnp.concatenate([kb[s], kb[s + 1]], axis=0)) for s in subs]
    sink = sink_ref[...]
    p, denom = [], []
    for s in subs:
        x = sc[s] + (bias_first_ref[...] if s == 0 else bias_ref[...])
        mx = jnp.maximum(jnp.max(x, axis=-1, keepdims=True), sink)
        e = jnp.exp(x - jnp.concatenate([mx, mx], axis=1))
        p.append(e)
        denom.append(jnp.sum(e, axis=-1, keepdims=True) + jnp.exp(sink - mx))
    pv = [_bdot(p[s], jnp.concatenate([vb[s], vb[s + 1]], axis=0)) for s in subs]
    for s in subs:
        o = pv[s] / denom[s]
        for g in range(SWA_GROUP):
            o_ref[s * blk:(s + 1) * blk, g * HEAD_DIM:(g + 1) * HEAD_DIM] = (
                o[g * blk:(g + 1) * blk].astype(o_ref.dtype))


def _swa(proj, sinks, q_norm_w, k_norm_w, *, batch, seq, q_col, k_col, v_col, nq):
    m = proj.shape[0]
    blk = WINDOW
    nq = min(nq, seq // blk)
    steps = seq // (nq * blk)
    gw = SWA_GROUP * HEAD_DIM

    def cur(col):
        return lambda b, h, i: (b * steps + i, col + h)

    def prev(col):
        return lambda b, h, i: (b * steps * nq + jnp.maximum(i * nq - 1, 0), col + h)

    wspec = pl.BlockSpec((1, HEAD_DIM), lambda b, h, i: (0, 0))
    table = pltpu.VMEM((SWA_GROUP * blk, 2 * blk), F32)
    return pl.pallas_call(
        functools.partial(_swa_kernel, nq=nq),
        grid=(batch, SWA_KV_HEADS, steps),
        in_specs=[pl.BlockSpec(memory_space=pltpu.SMEM),
                  pl.BlockSpec((nq * blk, gw), lambda b, h, i: (b * steps + i, q_col // SWA_GROUP + h)),
                  pl.BlockSpec((blk, HEAD_DIM), prev(k_col)), pl.BlockSpec((nq * blk, HEAD_DIM), cur(k_col)),
                  pl.BlockSpec((blk, HEAD_DIM), prev(v_col)), pl.BlockSpec((nq * blk, HEAD_DIM), cur(v_col)),
                  wspec, wspec],
        out_specs=pl.BlockSpec((nq * blk, gw), lambda b, h, i: (b * steps + i, h)),
        out_shape=jax.ShapeDtypeStruct((m, SWA_Q_HEADS * HEAD_DIM), BF16),
        scratch_shapes=[table, table, pltpu.VMEM((SWA_GROUP * blk, LANES), F32)],
        compiler_params=_params("parallel", "parallel", "arbitrary"),
        name="sliding_window_gqa",
    )(sinks, proj, proj, proj, proj, proj, q_norm_w, k_norm_w)


def _pad_lanes(v):
    return jnp.pad(v.astype(F32), (0, LANES - v.shape[0])).reshape(1, LANES)


def _layer(x2, batch, seq, attn_norm_w, w_in, conv_w, a_log, dt_bias, dn_norm_w, q_norm_w, k_norm_w,
           sinks, w_out, ffn_norm_w, w_gate, w_up, w_down):
    m, d = x2.shape
    dn_w = DN_HEADS * HEAD_DIM
    q_w = SWA_Q_HEADS * HEAD_DIM
    kv_w = SWA_KV_HEADS * HEAD_DIM
    gates_at = 4 * dn_w
    n_gates = 2 * DN_HEADS

    w_main, w_ab = _split_w_in(w_in, gates_at=gates_at, n_gates=n_gates, tr=256)
    f = w_gate.shape[1]
    f_pad = -(-f // 512) * 512
    wg = _cast_pad_cols(w_gate, f_pad, tr=256, name="cast_w_gate")
    wu = _cast_pad_cols(w_up, f_pad, tr=256, name="cast_w_up")
    wd = _cast_pad_rows(w_down, f_pad, tr=256, name="cast_w_down")
    wo = _cast_pad_cols(w_out, w_out.shape[1], tr=512, name="cast_w_out")

    h = _rmsnorm(x2, attn_norm_w)
    proj = _matmul(h, w_main, tm=1024, tn=512, tk=d, out_dtype=F32, name="in_proj")
    ab = _matmul(h, w_ab, tm=1024, tn=LANES, tk=d, out_dtype=F32, name="gate_proj")

    o_dn = _gdn(proj, ab, conv_w, _pad_lanes(a_log), _pad_lanes(dt_bias), dn_norm_w.reshape(1, HEAD_DIM),
                batch=batch, seq=seq, tb=min(512, seq))
    sec = 4 * dn_w // HEAD_DIM
    o_sw = _swa(proj, sinks.astype(F32), q_norm_w.reshape(1, HEAD_DIM), k_norm_w.reshape(1, HEAD_DIM),
                batch=batch, seq=seq, q_col=sec, k_col=sec + q_w // HEAD_DIM,
                v_col=sec + (q_w + kv_w) // HEAD_DIM, nq=4)

    x1 = _matmul_pair(o_dn, o_sw, wo, x2, tm=1024, tn=512, name="out_proj")

    h2 = _rmsnorm(x1, ffn_norm_w)
    act = _gate_up(h2, wg, wu, tm=1024, tn=512)
    return _matmul(act, wd, tm=1024, tn=1024, tk=f_pad // 4, out_dtype=F32, residual=x1, name="down_proj")


def kernel(x, attn_norm_w, w_in, conv_w, a_log, dt_bias, dn_norm_w, q_norm_w, k_norm_w, sinks, w_out,
           ffn_norm_w, w_gate, w_up, w_down):
    batch, seq, d = x.shape
    x2 = x.reshape(batch * seq, d)
    for l in range(attn_norm_w.shape[0]):
        x2 = _layer(x2, batch, seq, attn_norm_w[l], w_in[l], conv_w[l], a_log[l], dt_bias[l], dn_norm_w[l],
                    q_norm_w[l], k_norm_w[l], sinks[l], w_out[l], ffn_norm_w[l], w_gate[l], w_up[l], w_down[l])
    return x2.reshape(batch, seq, d)
```

```python
import functools

import jax
import jax.numpy as jnp
from jax import lax
from jax.experimental import pallas as pl
from jax.experimental.pallas import tpu as pltpu

NORM_EPS = 1e-6
HEAD_DIM = 128
DN_HEADS = 16
CHUNK = 64
CONV_WIDTH = 4
SWA_Q_HEADS = 16
SWA_KV_HEADS = 4
SWA_GROUP = SWA_Q_HEADS // SWA_KV_HEADS
WINDOW = 128
LANES = 128
VMEM_LIMIT_BYTES = 56 * 1024 * 1024

BF16 = jnp.bfloat16
F32 = jnp.float32


def _params(*semantics):
    return pltpu.CompilerParams(dimension_semantics=semantics, vmem_limit_bytes=VMEM_LIMIT_BYTES)


def _rmsnorm_kernel(x_ref, w_ref, o_ref):
    x = x_ref[...]
    ms = jnp.mean(x * x, axis=-1, keepdims=True)
    o_ref[...] = (x * lax.rsqrt(ms + NORM_EPS) * w_ref[...]).astype(o_ref.dtype)


def _rmsnorm(x, w, tm=256):
    m, d = x.shape
    tm = min(tm, m)
    return pl.pallas_call(
        _rmsnorm_kernel,
        grid=(m // tm,),
        in_specs=[pl.BlockSpec((tm, d), lambda i: (i, 0)), pl.BlockSpec((1, d), lambda i: (0, 0))],
        out_specs=pl.BlockSpec((tm, d), lambda i: (i, 0)),
        out_shape=jax.ShapeDtypeStruct((m, d), BF16),
        compiler_params=_params("parallel"),
        name="rmsnorm",
    )(x, w.reshape(1, d))


def _cast_cols_kernel(w_ref, o_ref):
    n_in = w_ref.shape[1]
    o_ref[:, 0:n_in] = w_ref[...].astype(o_ref.dtype)
    if o_ref.shape[1] > n_in:
        o_ref[:, n_in:] = jnp.zeros((o_ref.shape[0], o_ref.shape[1] - n_in), o_ref.dtype)


def _cast_pad_cols(w, n_out, *, tr, name):
    k, n = w.shape
    return pl.pallas_call(
        _cast_cols_kernel,
        grid=(k // tr,),
        in_specs=[pl.BlockSpec((tr, n), lambda i: (i, 0))],
        out_specs=pl.BlockSpec((tr, n_out), lambda i: (i, 0)),
        out_shape=jax.ShapeDtypeStruct((k, n_out), BF16),
        compiler_params=_params("parallel"),
        name=name,
    )(w)


def _cast_rows_kernel(w_ref, o_ref, *, n_valid):
    i = pl.program_id(0)

    @pl.when(i < n_valid)
    def _():
        o_ref[...] = w_ref[...].astype(o_ref.dtype)

    @pl.when(i >= n_valid)
    def _():
        o_ref[...] = jnp.zeros_like(o_ref)


def _cast_pad_rows(w, k_out, *, tr, name):
    k, n = w.shape
    n_valid = k // tr
    return pl.pallas_call(
        functools.partial(_cast_rows_kernel, n_valid=n_valid),
        grid=(k_out // tr,),
        in_specs=[pl.BlockSpec((tr, n), lambda i: (jnp.minimum(i, n_valid - 1), 0))],
        out_specs=pl.BlockSpec((tr, n), lambda i: (i, 0)),
        out_shape=jax.ShapeDtypeStruct((k_out, n), BF16),
        compiler_params=_params("parallel"),
        name=name,
    )(w)


def _split_main_kernel(a_ref, b_ref, main_ref, *, n_aligned, n_gates):
    j = pl.program_id(0)

    @pl.when(j < n_aligned)
    def _():
        main_ref[...] = a_ref[...].T.astype(BF16)

    @pl.when(j >= n_aligned)
    def _():
        src = jnp.concatenate([a_ref[n_gates:, :], b_ref[...]], axis=0)
        main_ref[...] = src.T.astype(BF16)


def _split_gates_kernel(g_ref, gates_ref, *, n_gates):
    lane = lax.broadcasted_iota(jnp.int32, gates_ref.shape, 1)
    gates_ref[...] = jnp.where(lane < n_gates, g_ref[...].T, 0.0).astype(BF16)


def _split_w_in(w_in_t, *, gates_at, n_gates, tn):
    n, k = w_in_t.shape
    n_main = n - n_gates
    main = pl.pallas_call(
        functools.partial(_split_main_kernel, n_aligned=gates_at // tn, n_gates=n_gates),
        grid=(n_main // tn,),
        in_specs=[pl.BlockSpec((tn, k), lambda j: (j, 0)),
                  pl.BlockSpec((n_gates, k), lambda j: ((j + 1) * (tn // n_gates), 0))],
        out_specs=pl.BlockSpec((k, tn), lambda j: (0, j)),
        out_shape=jax.ShapeDtypeStruct((k, n_main), BF16),
        compiler_params=_params("parallel"),
        name="split_w_in",
    )(w_in_t, w_in_t)
    gates = pl.pallas_call(
        functools.partial(_split_gates_kernel, n_gates=n_gates),
        grid=(1,),
        in_specs=[pl.BlockSpec((LANES, k), lambda j: (gates_at // LANES, 0))],
        out_specs=pl.BlockSpec((k, LANES), lambda j: (0, 0)),
        out_shape=jax.ShapeDtypeStruct((k, LANES), BF16),
        compiler_params=_params("arbitrary"),
        name="split_w_gates",
    )(w_in_t)
    return main, gates


def _mm_kernel(a_ref, w_ref, *rest, nk, residual):
    r_ref = rest[0] if residual else None
    o_ref = rest[1 if residual else 0]
    part = jnp.dot(a_ref[...], w_ref[...], preferred_element_type=F32)
    if nk == 1:
        if residual:
            part = part + r_ref[...]
        o_ref[...] = part.astype(o_ref.dtype)
        return
    acc_ref = rest[-1]
    k = pl.program_id(2)

    @pl.when(k == 0)
    def _():
        acc_ref[...] = part

    @pl.when(k > 0)
    def _():
        acc_ref[...] += part

    @pl.when(k == nk - 1)
    def _():
        out = acc_ref[...]
        if residual:
            out = out + r_ref[...]
        o_ref[...] = out.astype(o_ref.dtype)


def _matmul(a, w, *, tm, tn, tk, out_dtype, residual=None, name):
    m, kd = a.shape
    n = w.shape[1]
    tm = min(tm, m)
    nk = kd // tk
    in_specs = [pl.BlockSpec((tm, tk), lambda i, j, k: (i, k)),
                pl.BlockSpec((tk, tn), lambda i, j, k: (k, j))]
    args = [a, w]
    if residual is not None:
        in_specs.append(pl.BlockSpec((tm, tn), lambda i, j, k: (i, j)))
        args.append(residual)
    return pl.pallas_call(
        functools.partial(_mm_kernel, nk=nk, residual=residual is not None),
        grid=(m // tm, n // tn, nk),
        in_specs=in_specs,
        out_specs=pl.BlockSpec((tm, tn), lambda i, j, k: (i, j)),
        out_shape=jax.ShapeDtypeStruct((m, n), out_dtype),
        scratch_shapes=[pltpu.VMEM((tm, tn), F32)] if nk > 1 else [],
        compiler_params=_params("parallel", "parallel", "arbitrary"),
        name=name,
    )(*args)


def _mm_pair_kernel(a1_ref, a2_ref, w_ref, r_ref, o_ref):
    k1 = a1_ref.shape[1]
    acc = jnp.dot(a1_ref[...], w_ref[0:k1, :], preferred_element_type=F32)
    acc = acc + jnp.dot(a2_ref[...], w_ref[k1:, :], preferred_element_type=F32)
    o_ref[...] = (acc + r_ref[...]).astype(o_ref.dtype)


def _matmul_pair(a1, a2, w, residual, *, tm, tn, name):
    m, k1 = a1.shape
    k2 = a2.shape[1]
    n = w.shape[1]
    tm = min(tm, m)
    return pl.pallas_call(
        _mm_pair_kernel,
        grid=(m // tm, n // tn),
        in_specs=[pl.BlockSpec((tm, k1), lambda i, j: (i, 0)),
                  pl.BlockSpec((tm, k2), lambda i, j: (i, 0)),
                  pl.BlockSpec((k1 + k2, tn), lambda i, j: (0, j)),
                  pl.BlockSpec((tm, tn), lambda i, j: (i, j))],
        out_specs=pl.BlockSpec((tm, tn), lambda i, j: (i, j)),
        out_shape=jax.ShapeDtypeStruct((m, n), F32),
        compiler_params=_params("parallel", "parallel"),
        name=name,
    )(a1, a2, w, residual)


def _gate_up_kernel(a_ref, wg_ref, wu_ref, o_ref):
    a = a_ref[...]
    g = jnp.dot(a, wg_ref[...], preferred_element_type=F32)
    u = jnp.dot(a, wu_ref[...], preferred_element_type=F32)
    o_ref[...] = (g * jax.nn.sigmoid(g) * u).astype(o_ref.dtype)


def _gate_up(a, wg, wu, *, tm, tn):
    m, kd = a.shape
    n = wg.shape[1]
    tm = min(tm, m)
    return pl.pallas_call(
        _gate_up_kernel,
        grid=(m // tm, n // tn),
        in_specs=[pl.BlockSpec((tm, kd), lambda i, j: (i, 0)),
                  pl.BlockSpec((kd, tn), lambda i, j: (0, j)),
                  pl.BlockSpec((kd, tn), lambda i, j: (0, j))],
        out_specs=pl.BlockSpec((tm, tn), lambda i, j: (i, j)),
        out_shape=jax.ShapeDtypeStruct((m, n), BF16),
        compiler_params=_params("parallel", "parallel"),
        name="gate_up",
    )(a, wg, wu)


def _bdot(a, b):
    return jnp.dot(a.astype(BF16), b.astype(BF16), preferred_element_type=F32)


def _bdot_nt(a, b):
    return lax.dot_general(a.astype(BF16), b.astype(BF16), (((1,), (1,)), ((), ())),
                           preferred_element_type=F32)


def _bdot_tn(a, b):
    return lax.dot_general(a.astype(BF16), b.astype(BF16), (((0,), (0,)), ((), ())),
                           preferred_element_type=F32)


def _xdot(a, b):
    return jnp.dot(a, b, preferred_element_type=F32, precision=lax.Precision.HIGHEST)


def _gdn_kernel(q_ref, k_ref, v_ref, z_ref, ab_ref, cw_ref, alog_ref, dtb_ref, normw_ref,
                o_ref, tail_ref, s_ref, *, tb):
    t = pl.program_id(1)
    nh = DN_HEADS
    c = nh * HEAD_DIM
    pad = 8

    @pl.when(t == 0)
    def _():
        tail_ref[...] = jnp.zeros_like(tail_ref)
        s_ref[...] = jnp.zeros_like(s_ref)

    sections = (q_ref, k_ref, v_ref)

    ri = lax.broadcasted_iota(jnp.int32, (CHUNK, LANES), 0)
    lane = lax.broadcasted_iota(jnp.int32, (CHUNK, LANES), 1)
    ci = lane & (CHUNK - 1)
    first = lane < CHUNK
    first2 = lax.broadcasted_iota(jnp.int32, (2 * CHUNK, LANES), 1) < CHUNK
    top2 = lax.broadcasted_iota(jnp.int32, (2 * CHUNK, LANES), 0) < CHUNK
    incl = ri >= ci
    strict = ri > ci
    eye_f = jnp.where(ri == ci, 1.0, 0.0).astype(F32)
    ri_sq = lax.broadcasted_iota(jnp.int32, (CHUNK, CHUNK), 0)
    ci_sq = lax.broadcasted_iota(jnp.int32, (CHUNK, CHUNK), 1)
    tri_ones = jnp.where(ri_sq >= ci_sq, 1.0, 0.0).astype(F32)
    level_masks = []
    for lvl in range(CHUNK.bit_length() - 1):
        same_pair = (ri >> (lvl + 1)) == (ci >> (lvl + 1))
        level_masks.append(same_pair & (((ri >> lvl) & 1) == 1) & (((ci >> lvl) & 1) == 0))
    alog = alog_ref[...]
    dtb = dtb_ref[...]
    normw = normw_ref[...]
    scale = HEAD_DIM ** -0.5
    heads = range(nh)
    pairs = range(nh // 2)
    zeros_hd = jnp.zeros((CHUNK, HEAD_DIM), F32)

    def block_diag(m):
        mb = jnp.concatenate([m, m], axis=0).astype(BF16)
        return jnp.where(first2 == top2, mb, jnp.zeros_like(mb))

    def lanecol(x, h):
        return x[:, h:h + 1]

    ones_hd = jnp.ones((HEAD_DIM, HEAD_DIM), BF16)

    def lane_sum(x):
        return jnp.dot(x.astype(BF16), ones_hd, preferred_element_type=F32)

    group = 2
    members = range(group)

    def group_body(it, carry):
        base = it * (group * CHUNK)
        rows = [pl.ds(pl.multiple_of(base + j * CHUNK, CHUNK), CHUNK) for j in members]

        conv = {}

        def sl(j, section, h):
            if (0, section, h) not in conv:
                lo = section * c + h * HEAD_DIM
                w = cw_ref[:, lo:lo + HEAD_DIM]
                prev = tail_ref[:, lo:lo + HEAD_DIM]
                for m in members:
                    x = sections[section][rows[m], h * HEAD_DIM:(h + 1) * HEAD_DIM]
                    ext = jnp.concatenate([prev, x], axis=0)
                    acc = x * w[CONV_WIDTH - 1:CONV_WIDTH, :]
                    for shift in range(1, CONV_WIDTH):
                        tap = CONV_WIDTH - 1 - shift
                        acc = acc + pltpu.roll(ext, shift, axis=0)[pad:pad + CHUNK, :] * w[tap:tap + 1, :]
                    conv[m, section, h] = acc * jax.nn.sigmoid(acc)
                    prev = x[CHUNK - pad:CHUNK, :]
                tail_ref[:, lo:lo + HEAD_DIM] = prev
            return conv[j, section, h]

        beta_all, gc_all, eg_all, tail_all, gl_all, gc_t = [], [], [], [], [], []
        for j in members:
            ab = ab_ref[rows[j], :]
            g = -jnp.exp(alog) * jax.nn.softplus(ab + dtb)
            beta_all.append(jax.nn.sigmoid(ab))
            gc = _xdot(tri_ones, g)
            gc_last = gc[CHUNK - 1:CHUNK, :]
            gc_all.append(gc)
            eg_all.append(jnp.exp(gc))
            tail_all.append(jnp.exp(gc_last - gc))
            gl_all.append(jnp.exp(gc_last))
            gc_t.append(jnp.concatenate([gc, pltpu.roll(gc, LANES - 1, axis=1)], axis=0).T)

        qn = [[None] * nh for _ in members]
        kn = [[None] * nh for _ in members]
        beta = [[None] * nh for _ in members]
        for j in members:
            for h in heads:
                qh = sl(j, 0, h)
                kh = sl(j, 1, h)
                qn[j][h] = qh * lax.rsqrt(lane_sum(qh * qh) + NORM_EPS) * scale
                kn[j][h] = kh * lax.rsqrt(lane_sum(kh * kh) + NORM_EPS)
                beta[j][h] = lanecol(beta_all[j], DN_HEADS + h)

        units = [(j, p) for j in members for p in pairs]
        qk, a_mat = {}, {}
        for j, p in units:
            h0, h1 = 2 * p, 2 * p + 1
            gc_col = jnp.where(first, lanecol(gc_all[j], h0), lanecol(gc_all[j], h1))
            decay = jnp.exp(jnp.where(incl, gc_col - gc_t[j][h0:h0 + 1, :], -jnp.inf))
            beta_p = jnp.where(first, beta[j][h0], beta[j][h1])
            lhs = jnp.concatenate([jnp.concatenate([qn[j][h0], qn[j][h1]], axis=1),
                                   jnp.concatenate([kn[j][h0], kn[j][h1]], axis=1)], axis=0)
            rhs = jnp.concatenate([jnp.concatenate([kn[j][h0], zeros_hd], axis=1),
                                   jnp.concatenate([zeros_hd, kn[j][h1]], axis=1)], axis=0)
            sc = _bdot_nt(lhs, rhs)
            qk[j, p] = sc[0:CHUNK] * decay
            a_mat[j, p] = jnp.where(strict, sc[CHUNK:2 * CHUNK] * beta_p * decay, 0.0)

        x_inv = {u: eye_f - jnp.where(level_masks[0], a_mat[u], 0.0) for u in units}
        for mask in level_masks[1:]:
            y = {u: jnp.dot(jnp.where(mask, a_mat[u], 0.0).astype(BF16), block_diag(x_inv[u]),
                            preferred_element_type=F32) for u in units}
            xy = {u: jnp.dot(x_inv[u].astype(BF16), block_diag(y[u]), preferred_element_type=F32) for u in units}
            x_inv = {u: x_inv[u] - xy[u] for u in units}

        uw = [[None] * nh for _ in members]
        for j, p in units:
            rhs = []
            for h in (2 * p, 2 * p + 1):
                kb_eg = kn[j][h] * (beta[j][h] * lanecol(eg_all[j], h))
                rhs.append(jnp.concatenate([sl(j, 2, h) * beta[j][h], kb_eg], axis=1))
            both = jnp.dot(block_diag(x_inv[j, p]), jnp.concatenate(rhs, axis=0).astype(BF16),
                           preferred_element_type=F32)
            uw[j][2 * p] = both[0:CHUNK]
            uw[j][2 * p + 1] = both[CHUNK:2 * CHUNK]

        for j in members:
            ws = []
            for h in heads:
                qd = qn[j][h] * lanecol(eg_all[j], h)
                ws.append(_bdot(jnp.concatenate([uw[j][h][:, HEAD_DIM:2 * HEAD_DIM], qd], axis=0), s_ref[h]))
            v_new = [uw[j][h][:, 0:HEAD_DIM] - ws[h][0:CHUNK] for h in heads]
            o_intra = [None] * nh
            for p in pairs:
                both = jnp.dot(block_diag(qk[j, p]),
                               jnp.concatenate([v_new[2 * p], v_new[2 * p + 1]], axis=0).astype(BF16),
                               preferred_element_type=F32)
                o_intra[2 * p] = both[0:CHUNK]
                o_intra[2 * p + 1] = both[CHUNK:2 * CHUNK]
            s_add = [_bdot_tn(kn[j][h] * lanecol(tail_all[j], h), v_new[h]) for h in heads]
            for h in heads:
                s_ref[h] = s_ref[h] * lanecol(gl_all[j], h) + s_add[h]
                o = ws[h][CHUNK:2 * CHUNK] + o_intra[h]
                zh = z_ref[rows[j], h * HEAD_DIM:(h + 1) * HEAD_DIM]
                o = o * lax.rsqrt(jnp.mean(o * o, axis=-1, keepdims=True) + NORM_EPS) * normw
                o_ref[rows[j], h * HEAD_DIM:(h + 1) * HEAD_DIM] = (
                    o * (zh * jax.nn.sigmoid(zh))).astype(o_ref.dtype)
        return carry

    lax.fori_loop(0, tb // (group * CHUNK), group_body, 0)


def _gdn(proj, ab, conv_w, alog_lane, dtb_lane, norm_w, *, batch, seq, tb):
    m = proj.shape[0]
    c = DN_HEADS * HEAD_DIM
    nt = seq // tb

    def col(section):
        return lambda b, t: (b * nt + t, section)

    lane_spec = pl.BlockSpec((1, LANES), lambda b, t: (0, 0))
    return pl.pallas_call(
        functools.partial(_gdn_kernel, tb=tb),
        grid=(batch, nt),
        in_specs=[pl.BlockSpec((tb, c), col(0)), pl.BlockSpec((tb, c), col(1)),
                  pl.BlockSpec((tb, c), col(2)), pl.BlockSpec((tb, c), col(3)),
                  pl.BlockSpec((tb, LANES), lambda b, t: (b * nt + t, 0)),
                  pl.BlockSpec((CONV_WIDTH, 3 * c), lambda b, t: (0, 0)),
                  lane_spec, lane_spec, lane_spec],
        out_specs=pl.BlockSpec((tb, c), lambda b, t: (b * nt + t, 0)),
        out_shape=jax.ShapeDtypeStruct((m, c), BF16),
        scratch_shapes=[pltpu.VMEM((8, 3 * c), F32), pltpu.VMEM((DN_HEADS, HEAD_DIM, HEAD_DIM), F32)],
        compiler_params=_params("parallel", "arbitrary"),
        name="gated_delta_rule",
    )(proj, proj, proj, proj, ab, conv_w, alog_lane, dtb_lane, norm_w)


def _swa_kernel(sinks_ref, q_ref, kp_ref, kc_ref, vp_ref, vc_ref, qw_ref, kw_ref, o_ref,
                bias_ref, bias_first_ref, sink_ref, *, nq):
    kvh = pl.program_id(1)
    i = pl.program_id(2)
    blk = WINDOW
    rows = SWA_GROUP * blk

    @pl.when(i == 0)
    def _():
        ri = lax.broadcasted_iota(jnp.int32, (rows, 2 * blk), 0)
        ki = lax.broadcasted_iota(jnp.int32, (rows, 2 * blk), 1)
        dist = (ri & (blk - 1)) + blk - ki
        head = kvh * SWA_GROUP + (ri >> (blk.bit_length() - 1))
        slope = jnp.exp2(-8.0 * (head + 1).astype(F32) / SWA_Q_HEADS)
        bias = jnp.where((dist >= 0) & (dist < WINDOW), -slope * dist.astype(F32), -jnp.inf)
        bias_ref[...] = bias
        bias_first_ref[...] = jnp.where(ki >= blk, bias, -jnp.inf)
        for g in range(SWA_GROUP):
            sink_ref[g * blk:(g + 1) * blk, :] = jnp.full((blk, LANES), sinks_ref[kvh * SWA_GROUP + g], F32)

    @pl.when(i == 1)
    def _():
        bias_first_ref[...] = bias_ref[...]

    def rms(x, w):
        return x * lax.rsqrt(jnp.mean(x * x, axis=-1, keepdims=True) + NORM_EPS) * w

    qw = qw_ref[...] * (HEAD_DIM ** -0.5)
    kw = kw_ref[...]
    subs = range(nq)
    kb = [rms(kp_ref[...], kw)] + [rms(kc_ref[s * blk:(s + 1) * blk, :], kw) for s in subs]
    vb = [vp_ref[...]] + [vc_ref[s * blk:(s + 1) * blk, :] for s in subs]
    q = [jnp.concatenate([rms(q_ref[s * blk:(s + 1) * blk, g * HEAD_DIM:(g + 1) * HEAD_DIM], qw)
                          for g in range(SWA_GROUP)], axis=0) for s in subs]
    sc = [_bdot_nt(q[s], jnp.concatenate([kb[s], kb[s + 1]], axis=0)) for s in subs]
    sink = sink_ref[...]
    p, denom = [], []
    for s in subs:
        x = sc[s] + (bias_first_ref[...] if s == 0 else bias_ref[...])
        mx = jnp.maximum(jnp.max(x, axis=-1, keepdims=True), sink)
        e = jnp.exp(x - jnp.concatenate([mx, mx], axis=1))
        p.append(e)
        denom.append(jnp.sum(e, axis=-1, keepdims=True) + jnp.exp(sink - mx))
    pv = [_bdot(p[s], jnp.concatenate([vb[s], vb[s + 1]], axis=0)) for s in subs]
    for s in subs:
        o = pv[s] / denom[s]
        for g in range(SWA_GROUP):
            o_ref[s * blk:(s + 1) * blk, g * HEAD_DIM:(g + 1) * HEAD_DIM] = (
                o[g * blk:(g + 1) * blk].astype(o_ref.dtype))


def _swa(proj, sinks, q_norm_w, k_norm_w, *, batch, seq, q_col, k_col, v_col, nq):
    m = proj.shape[0]
    blk = WINDOW
    nq = min(nq, seq // blk)
    steps = seq // (nq * blk)
    gw = SWA_GROUP * HEAD_DIM

    def cur(col):
        return lambda b, h, i: (b * steps + i, col + h)

    def prev(col):
        return lambda b, h, i: (b * steps * nq + jnp.maximum(i * nq - 1, 0), col + h)

    wspec = pl.BlockSpec((1, HEAD_DIM), lambda b, h, i: (0, 0))
    table = pltpu.VMEM((SWA_GROUP * blk, 2 * blk), F32)
    return pl.pallas_call(
        functools.partial(_swa_kernel, nq=nq),
        grid=(batch, SWA_KV_HEADS, steps),
        in_specs=[pl.BlockSpec(memory_space=pltpu.SMEM),
                  pl.BlockSpec((nq * blk, gw), lambda b, h, i: (b * steps + i, q_col // SWA_GROUP + h)),
                  pl.BlockSpec((blk, HEAD_DIM), prev(k_col)), pl.BlockSpec((nq * blk, HEAD_DIM), cur(k_col)),
                  pl.BlockSpec((blk, HEAD_DIM), prev(v_col)), pl.BlockSpec((nq * blk, HEAD_DIM), cur(v_col)),
                  wspec, wspec],
        out_specs=pl.BlockSpec((nq * blk, gw), lambda b, h, i: (b * steps + i, h)),
        out_shape=jax.ShapeDtypeStruct((m, SWA_Q_HEADS * HEAD_DIM), BF16),
        scratch_shapes=[table, table, pltpu.VMEM((SWA_GROUP * blk, LANES), F32)],
        compiler_params=_params("parallel", "parallel", "arbitrary"),
        name="sliding_window_gqa",
    )(sinks, proj, proj, proj, proj, proj, q_norm_w, k_norm_w)


def _pad_lanes(v):
    return jnp.pad(v.astype(F32), (0, LANES - v.shape[0])).reshape(1, LANES)


def _layer(x2, batch, seq, attn_norm_w, w_in, conv_w, a_log, dt_bias, dn_norm_w, q_norm_w, k_norm_w,
           sinks, w_out, ffn_norm_w, w_gate, w_up, w_down):
    m, d = x2.shape
    dn_w = DN_HEADS * HEAD_DIM
    q_w = SWA_Q_HEADS * HEAD_DIM
    kv_w = SWA_KV_HEADS * HEAD_DIM
    gates_at = 4 * dn_w
    n_gates = 2 * DN_HEADS

    w_main, w_ab = _split_w_in(w_in.T, gates_at=gates_at, n_gates=n_gates, tn=512)
    f = w_gate.shape[1]
    f_pad = -(-f // 512) * 512
    wg = _cast_pad_cols(w_gate, f_pad, tr=256, name="cast_w_gate")
    wu = _cast_pad_cols(w_up, f_pad, tr=256, name="cast_w_up")
    wd = _cast_pad_rows(w_down, f_pad, tr=256, name="cast_w_down")
    wo = _cast_pad_cols(w_out, w_out.shape[1], tr=512, name="cast_w_out")

    h = _rmsnorm(x2, attn_norm_w)
    proj = _matmul(h, w_main, tm=1024, tn=1024, tk=d, out_dtype=F32, name="in_proj")
    ab = _matmul(h, w_ab, tm=1024, tn=LANES, tk=d, out_dtype=F32, name="gate_proj")

    o_dn = _gdn(proj, ab, conv_w, _pad_lanes(a_log), _pad_lanes(dt_bias), dn_norm_w.reshape(1, HEAD_DIM),
                batch=batch, seq=seq, tb=min(512, seq))
    sec = 4 * dn_w // HEAD_DIM
    o_sw = _swa(proj, sinks.astype(F32), q_norm_w.reshape(1, HEAD_DIM), k_norm_w.reshape(1, HEAD_DIM),
                batch=batch, seq=seq, q_col=sec, k_col=sec + q_w // HEAD_DIM,
                v_col=sec + (q_w + kv_w) // HEAD_DIM, nq=4)

    x1 = _matmul_pair(o_dn, o_sw, wo, x2, tm=1024, tn=1024, name="out_proj")

    h2 = _rmsnorm(x1, ffn_norm_w)
    act = _gate_up(h2, wg, wu, tm=1024, tn=512)
    return _matmul(act, wd, tm=1024, tn=1024, tk=f_pad // 4, out_dtype=F32, residual=x1, name="down_proj")


def kernel(x, attn_norm_w, w_in, conv_w, a_log, dt_bias, dn_norm_w, q_norm_w, k_norm_w, sinks, w_out,
           ffn_norm_w, w_gate, w_up, w_down):
    batch, seq, d = x.shape
    x2 = x.reshape(batch * seq, d)
    for l in range(attn_norm_w.shape[0]):
        x2 = _layer(x2, batch, seq, attn_norm_w[l], w_in[l], conv_w[l], a_log[l], dt_bias[l], dn_norm_w[l],
                    q_norm_w[l], k_norm_w[l], sinks[l], w_out[l], ffn_norm_w[l], w_gate[l], w_up[l], w_down[l])
    return x2.reshape(batch, seq, d)
```

```python
import functools
from typing import NamedTuple

import jax
import jax.numpy as jnp
from jax import lax
from jax.experimental import pallas as pl
from jax.experimental.pallas import tpu as pltpu

NORM_EPS = 1e-6
HEAD_DIM = 128
DN_HEADS = 16
CHUNK = 64
CONV_WIDTH = 4
SWA_Q_HEADS = 16
SWA_KV_HEADS = 4
SWA_GROUP = SWA_Q_HEADS // SWA_KV_HEADS
WINDOW = 128
LANES = 128
VMEM_LIMIT_BYTES = 56 * 1024 * 1024

BF16 = jnp.bfloat16
F32 = jnp.float32


def _params(*semantics):
    return pltpu.CompilerParams(dimension_semantics=semantics, vmem_limit_bytes=VMEM_LIMIT_BYTES)


def _rmsnorm_kernel(x_ref, w_ref, o_ref):
    x = x_ref[...]
    ms = jnp.mean(x * x, axis=-1, keepdims=True)
    o_ref[...] = (x * lax.rsqrt(ms + NORM_EPS) * w_ref[...]).astype(o_ref.dtype)


def _rmsnorm(x, w, tm=256):
    m, d = x.shape
    tm = min(tm, m)
    return pl.pallas_call(
        _rmsnorm_kernel,
        grid=(m // tm,),
        in_specs=[pl.BlockSpec((tm, d), lambda i: (i, 0)), pl.BlockSpec((1, d), lambda i: (0, 0))],
        out_specs=pl.BlockSpec((tm, d), lambda i: (i, 0)),
        out_shape=jax.ShapeDtypeStruct((m, d), BF16),
        compiler_params=_params("parallel"),
        name="rmsnorm",
    )(x, w.reshape(1, d))


def _cast_cols_kernel(w_ref, o_ref):
    n_in = w_ref.shape[1]
    o_ref[:, 0:n_in] = w_ref[...].astype(o_ref.dtype)
    if o_ref.shape[1] > n_in:
        o_ref[:, n_in:] = jnp.zeros((o_ref.shape[0], o_ref.shape[1] - n_in), o_ref.dtype)


def _cast_pad_cols(w, n_out, *, tr, name):
    k, n = w.shape
    return pl.pallas_call(
        _cast_cols_kernel,
        grid=(k // tr,),
        in_specs=[pl.BlockSpec((tr, n), lambda i: (i, 0))],
        out_specs=pl.BlockSpec((tr, n_out), lambda i: (i, 0)),
        out_shape=jax.ShapeDtypeStruct((k, n_out), BF16),
        compiler_params=_params("parallel"),
        name=name,
    )(w)


def _cast_rows_kernel(w_ref, o_ref, *, n_valid):
    i = pl.program_id(0)

    @pl.when(i < n_valid)
    def _():
        o_ref[...] = w_ref[...].astype(o_ref.dtype)

    @pl.when(i >= n_valid)
    def _():
        o_ref[...] = jnp.zeros_like(o_ref)


def _cast_pad_rows(w, k_out, *, tr, name):
    k, n = w.shape
    n_valid = k // tr
    return pl.pallas_call(
        functools.partial(_cast_rows_kernel, n_valid=n_valid),
        grid=(k_out // tr,),
        in_specs=[pl.BlockSpec((tr, n), lambda i: (jnp.minimum(i, n_valid - 1), 0))],
        out_specs=pl.BlockSpec((tr, n), lambda i: (i, 0)),
        out_shape=jax.ShapeDtypeStruct((k_out, n), BF16),
        compiler_params=_params("parallel"),
        name=name,
    )(w)


def _split_main_kernel(a_ref, b_ref, main_ref, *, n_aligned, n_gates):
    j = pl.program_id(0)

    @pl.when(j < n_aligned)
    def _():
        main_ref[...] = a_ref[...].T.astype(BF16)

    @pl.when(j >= n_aligned)
    def _():
        src = jnp.concatenate([a_ref[n_gates:, :], b_ref[...]], axis=0)
        main_ref[...] = src.T.astype(BF16)


def _split_gates_kernel(g_ref, gates_ref, *, n_gates):
    lane = lax.broadcasted_iota(jnp.int32, gates_ref.shape, 1)
    gates_ref[...] = jnp.where(lane < n_gates, g_ref[...].T, 0.0).astype(BF16)


def _split_w_in(w_in_t, *, gates_at, n_gates, tn):
    n, k = w_in_t.shape
    n_main = n - n_gates
    main = pl.pallas_call(
        functools.partial(_split_main_kernel, n_aligned=gates_at // tn, n_gates=n_gates),
        grid=(n_main // tn,),
        in_specs=[pl.BlockSpec((tn, k), lambda j: (j, 0)),
                  pl.BlockSpec((n_gates, k), lambda j: ((j + 1) * (tn // n_gates), 0))],
        out_specs=pl.BlockSpec((k, tn), lambda j: (0, j)),
        out_shape=jax.ShapeDtypeStruct((k, n_main), BF16),
        compiler_params=_params("parallel"),
        name="split_w_in",
    )(w_in_t, w_in_t)
    gates = pl.pallas_call(
        functools.partial(_split_gates_kernel, n_gates=n_gates),
        grid=(1,),
        in_specs=[pl.BlockSpec((LANES, k), lambda j: (gates_at // LANES, 0))],
        out_specs=pl.BlockSpec((k, LANES), lambda j: (0, 0)),
        out_shape=jax.ShapeDtypeStruct((k, LANES), BF16),
        compiler_params=_params("arbitrary"),
        name="split_w_gates",
    )(w_in_t)
    return main, gates


class _Rider(NamedTuple):
    src: jax.Array
    rows_out: int
    cols_out: int


def _rider_block_rows(rider, steps):
    rows = rider.src.shape[0]
    br = 32
    while br <= rows:
        if rows % br == 0 and rider.rows_out % br == 0 and rider.rows_out // br <= steps:
            return br
        br *= 2
    return None


def _rider_plumbing(riders, steps, step_of):
    in_specs, out_specs, out_shapes, meta = [], [], [], []
    for r in riders:
        rows, cols = r.src.shape
        br = _rider_block_rows(r, steps)
        n_valid, n_out = rows // br, r.rows_out // br
        in_specs.append(pl.BlockSpec((br, cols), lambda *g, n=n_valid: (jnp.minimum(step_of(*g), n - 1), 0)))
        out_specs.append(pl.BlockSpec((br, r.cols_out), lambda *g, n=n_out: (jnp.minimum(step_of(*g), n - 1), 0)))
        out_shapes.append(jax.ShapeDtypeStruct((r.rows_out, r.cols_out), BF16))
        meta.append((n_valid, n_out))
    return in_specs, out_specs, out_shapes, tuple(meta)


def _rider_cast(step, src_ref, dst_ref, n_valid, n_out):
    cols = src_ref.shape[1]

    @pl.when(step < n_valid)
    def _():
        dst_ref[:, 0:cols] = src_ref[...].astype(dst_ref.dtype)
        if dst_ref.shape[1] > cols:
            dst_ref[:, cols:] = jnp.zeros((dst_ref.shape[0], dst_ref.shape[1] - cols), dst_ref.dtype)

    if n_out > n_valid:
        @pl.when(jnp.logical_and(step >= n_valid, step < n_out))
        def _():
            dst_ref[...] = jnp.zeros_like(dst_ref)


def _mm_kernel(a_ref, w_ref, *rest, nk, residual, riders=()):
    nr = len(riders)
    r_ref = rest[0] if residual else None
    first = 1 if residual else 0
    rider_in = rest[first:first + nr]
    o_ref = rest[first + nr]
    rider_out = rest[first + nr + 1:first + 2 * nr + 1]
    if nr:
        step = pl.program_id(0) * pl.num_programs(1) + pl.program_id(1)
        for src_ref, dst_ref, (n_valid, n_out) in zip(rider_in, rider_out, riders):
            _rider_cast(step, src_ref, dst_ref, n_valid, n_out)
    part = jnp.dot(a_ref[...], w_ref[...], preferred_element_type=F32)
    if nk == 1:
        if residual:
            part = part + r_ref[...]
        o_ref[...] = part.astype(o_ref.dtype)
        return
    acc_ref = rest[-1]
    k = pl.program_id(2)

    @pl.when(k == 0)
    def _():
        acc_ref[...] = part

    @pl.when(k > 0)
    def _():
        acc_ref[...] += part

    @pl.when(k == nk - 1)
    def _():
        out = acc_ref[...]
        if residual:
            out = out + r_ref[...]
        o_ref[...] = out.astype(o_ref.dtype)


def _matmul(a, w, *, tm, tn, tk, out_dtype, residual=None, riders=(), name):
    m, kd = a.shape
    n = w.shape[1]
    tm = min(tm, m)
    nk = kd // tk
    nj = n // tn
    in_specs = [pl.BlockSpec((tm, tk), lambda i, j, k: (i, k)),
                pl.BlockSpec((tk, tn), lambda i, j, k: (k, j))]
    args = [a, w]
    if residual is not None:
        in_specs.append(pl.BlockSpec((tm, tn), lambda i, j, k: (i, j)))
        args.append(residual)
    out_specs = [pl.BlockSpec((tm, tn), lambda i, j, k: (i, j))]
    out_shape = [jax.ShapeDtypeStruct((m, n), out_dtype)]
    meta = ()
    if riders:
        assert nk == 1
        r_in, r_out, r_shapes, meta = _rider_plumbing(riders, (m // tm) * nj, lambda i, j, k: i * nj + j)
        in_specs += r_in
        args += [r.src for r in riders]
        out_specs += r_out
        out_shape += r_shapes
    outs = pl.pallas_call(
        functools.partial(_mm_kernel, nk=nk, residual=residual is not None, riders=meta),
        grid=(m // tm, nj, nk),
        in_specs=in_specs,
        out_specs=out_specs,
        out_shape=out_shape,
        scratch_shapes=[pltpu.VMEM((tm, tn), F32)] if nk > 1 else [],
        compiler_params=_params("arbitrary", "arbitrary", "arbitrary"),
        name=name,
    )(*args)
    return outs if riders else outs[0]


def _mm_pair_kernel(a1_ref, a2_ref, w_ref, r_ref, *rest, riders):
    nr = len(riders)
    o_ref = rest[nr]
    if nr:
        step = pl.program_id(0) * pl.num_programs(1) + pl.program_id(1)
        for src_ref, dst_ref, (n_valid, n_out) in zip(rest[:nr], rest[nr + 1:], riders):
            _rider_cast(step, src_ref, dst_ref, n_valid, n_out)
    k1 = a1_ref.shape[1]
    acc = jnp.dot(a1_ref[...], w_ref[0:k1, :], preferred_element_type=F32)
    acc = acc + jnp.dot(a2_ref[...], w_ref[k1:, :], preferred_element_type=F32)
    o_ref[...] = (acc + r_ref[...]).astype(o_ref.dtype)


def _matmul_pair(a1, a2, w, residual, *, tm, tn, riders=(), name):
    m, k1 = a1.shape
    k2 = a2.shape[1]
    n = w.shape[1]
    tm = min(tm, m)
    nj = n // tn
    r_in, r_out, r_shapes, meta = _rider_plumbing(riders, (m // tm) * nj, lambda i, j: i * nj + j)
    outs = pl.pallas_call(
        functools.partial(_mm_pair_kernel, riders=meta),
        grid=(m // tm, nj),
        in_specs=[pl.BlockSpec((tm, k1), lambda i, j: (i, 0)),
                  pl.BlockSpec((tm, k2), lambda i, j: (i, 0)),
                  pl.BlockSpec((k1 + k2, tn), lambda i, j: (0, j)),
                  pl.BlockSpec((tm, tn), lambda i, j: (i, j))] + r_in,
        out_specs=[pl.BlockSpec((tm, tn), lambda i, j: (i, j))] + r_out,
        out_shape=[jax.ShapeDtypeStruct((m, n), F32)] + r_shapes,
        compiler_params=_params("arbitrary", "arbitrary"),
        name=name,
    )(a1, a2, w, residual, *[r.src for r in riders])
    return outs if riders else outs[0]


def _gate_up_kernel(a_ref, wg_ref, wu_ref, *rest, riders):
    nr = len(riders)
    o_ref = rest[nr]
    if nr:
        step = pl.program_id(0) * pl.num_programs(1) + pl.program_id(1)
        for src_ref, dst_ref, (n_valid, n_out) in zip(rest[:nr], rest[nr + 1:], riders):
            _rider_cast(step, src_ref, dst_ref, n_valid, n_out)
    a = a_ref[...]
    g = jnp.dot(a, wg_ref[...], preferred_element_type=F32)
    u = jnp.dot(a, wu_ref[...], preferred_element_type=F32)
    o_ref[...] = (g * jax.nn.sigmoid(g) * u).astype(o_ref.dtype)


def _gate_up(a, wg, wu, *, tm, tn, riders=()):
    m, kd = a.shape
    n = wg.shape[1]
    tm = min(tm, m)
    nj = n // tn
    in_specs = [pl.BlockSpec((tm, kd), lambda i, j: (i, 0)),
                pl.BlockSpec((kd, tn), lambda i, j: (0, j)),
                pl.BlockSpec((kd, tn), lambda i, j: (0, j))]
    out_specs = [pl.BlockSpec((tm, tn), lambda i, j: (i, j))]
    out_shape = [jax.ShapeDtypeStruct((m, n), BF16)]
    r_in, r_out, r_shapes, meta = _rider_plumbing(riders, (m // tm) * nj, lambda i, j: i * nj + j)
    outs = pl.pallas_call(
        functools.partial(_gate_up_kernel, riders=meta),
        grid=(m // tm, nj),
        in_specs=in_specs + r_in,
        out_specs=out_specs + r_out,
        out_shape=out_shape + r_shapes,
        compiler_params=_params("arbitrary", "arbitrary"),
        name="gate_up",
    )(a, wg, wu, *[r.src for r in riders])
    return outs if riders else outs[0]


def _bdot(a, b):
    return jnp.dot(a.astype(BF16), b.astype(BF16), preferred_element_type=F32)


def _bdot_nt(a, b):
    return lax.dot_general(a.astype(BF16), b.astype(BF16), (((1,), (1,)), ((), ())),
                           preferred_element_type=F32)


def _bdot_tn(a, b):
    return lax.dot_general(a.astype(BF16), b.astype(BF16), (((0,), (0,)), ((), ())),
                           preferred_element_type=F32)


def _xdot(a, b):
    return jnp.dot(a, b, preferred_element_type=F32, precision=lax.Precision.HIGHEST)


def _gdn_kernel(q_ref, k_ref, v_ref, z_ref, ab_ref, cw_ref, alog_ref, dtb_ref, normw_ref,
                o_ref, tail_ref, s_ref, *, tb):
    t = pl.program_id(1)
    nh = DN_HEADS
    c = nh * HEAD_DIM
    pad = 8

    @pl.when(t == 0)
    def _():
        tail_ref[...] = jnp.zeros_like(tail_ref)
        s_ref[...] = jnp.zeros_like(s_ref)

    sections = (q_ref, k_ref, v_ref)

    ri = lax.broadcasted_iota(jnp.int32, (CHUNK, LANES), 0)
    lane = lax.broadcasted_iota(jnp.int32, (CHUNK, LANES), 1)
    ci = lane & (CHUNK - 1)
    first = lane < CHUNK
    first2 = lax.broadcasted_iota(jnp.int32, (2 * CHUNK, LANES), 1) < CHUNK
    top2 = lax.broadcasted_iota(jnp.int32, (2 * CHUNK, LANES), 0) < CHUNK
    incl = ri >= ci
    strict = ri > ci
    eye_f = jnp.where(ri == ci, 1.0, 0.0).astype(F32)
    ri_sq = lax.broadcasted_iota(jnp.int32, (CHUNK, CHUNK), 0)
    ci_sq = lax.broadcasted_iota(jnp.int32, (CHUNK, CHUNK), 1)
    tri_ones = jnp.where(ri_sq >= ci_sq, 1.0, 0.0).astype(F32)
    level_masks = []
    for lvl in range(CHUNK.bit_length() - 1):
        same_pair = (ri >> (lvl + 1)) == (ci >> (lvl + 1))
        level_masks.append(same_pair & (((ri >> lvl) & 1) == 1) & (((ci >> lvl) & 1) == 0))
    alog = alog_ref[...]
    dtb = dtb_ref[...]
    normw = normw_ref[...]
    scale = HEAD_DIM ** -0.5
    heads = range(nh)
    pairs = range(nh // 2)
    zeros_hd = jnp.zeros((CHUNK, HEAD_DIM), F32)

    def block_diag(m):
        mb = jnp.concatenate([m, m], axis=0).astype(BF16)
        return jnp.where(first2 == top2, mb, jnp.zeros_like(mb))

    def lanecol(x, h):
        return x[:, h:h + 1]

    ones_hd = jnp.ones((HEAD_DIM, HEAD_DIM), BF16)

    def lane_sum(x):
        return jnp.dot(x.astype(BF16), ones_hd, preferred_element_type=F32)

    group = 2
    members = range(group)

    def group_body(it, carry):
        base = it * (group * CHUNK)
        rows = [pl.ds(pl.multiple_of(base + j * CHUNK, CHUNK), CHUNK) for j in members]

        conv = {}

        def sl(j, section, h):
            if (0, section, h) not in conv:
                lo = section * c + h * HEAD_DIM
                w = cw_ref[:, lo:lo + HEAD_DIM]
                prev = tail_ref[:, lo:lo + HEAD_DIM]
                for m in members:
                    x = sections[section][rows[m], h * HEAD_DIM:(h + 1) * HEAD_DIM]
                    ext = jnp.concatenate([prev, x], axis=0)
                    acc = x * w[CONV_WIDTH - 1:CONV_WIDTH, :]
                    for shift in range(1, CONV_WIDTH):
                        tap = CONV_WIDTH - 1 - shift
                        acc = acc + pltpu.roll(ext, shift, axis=0)[pad:pad + CHUNK, :] * w[tap:tap + 1, :]
                    conv[m, section, h] = acc * jax.nn.sigmoid(acc)
                    prev = x[CHUNK - pad:CHUNK, :]
                tail_ref[:, lo:lo + HEAD_DIM] = prev
            return conv[j, section, h]

        beta_all, gc_all, eg_all, tail_all, gl_all, gc_t = [], [], [], [], [], []
        for j in members:
            ab = ab_ref[rows[j], :]
            g = -jnp.exp(alog) * jax.nn.softplus(ab + dtb)
            beta_all.append(jax.nn.sigmoid(ab))
            gc = _xdot(tri_ones, g)
            gc_last = gc[CHUNK - 1:CHUNK, :]
            gc_all.append(gc)
            eg_all.append(jnp.exp(gc))
            tail_all.append(jnp.exp(gc_last - gc))
            gl_all.append(jnp.exp(gc_last))
            gc_t.append(jnp.concatenate([gc, pltpu.roll(gc, LANES - 1, axis=1)], axis=0).T)

        qn = [[None] * nh for _ in members]
        kn = [[None] * nh for _ in members]
        beta = [[None] * nh for _ in members]
        for j in members:
            for h in heads:
                qh = sl(j, 0, h)
                kh = sl(j, 1, h)
                qn[j][h] = qh * lax.rsqrt(lane_sum(qh * qh) + NORM_EPS) * scale
                kn[j][h] = kh * lax.rsqrt(lane_sum(kh * kh) + NORM_EPS)
                beta[j][h] = lanecol(beta_all[j], DN_HEADS + h)

        units = [(j, p) for j in members for p in pairs]
        qk, a_mat = {}, {}
        for j, p in units:
            h0, h1 = 2 * p, 2 * p + 1
            gc_col = jnp.where(first, lanecol(gc_all[j], h0), lanecol(gc_all[j], h1))
            decay = jnp.exp(jnp.where(incl, gc_col - gc_t[j][h0:h0 + 1, :], -jnp.inf))
            beta_p = jnp.where(first, beta[j][h0], beta[j][h1])
            lhs = jnp.concatenate([jnp.concatenate([qn[j][h0], qn[j][h1]], axis=1),
                                   jnp.concatenate([kn[j][h0], kn[j][h1]], axis=1)], axis=0)
            rhs = jnp.concatenate([jnp.concatenate([kn[j][h0], zeros_hd], axis=1),
                                   jnp.concatenate([zeros_hd, kn[j][h1]], axis=1)], axis=0)
            sc = _bdot_nt(lhs, rhs)
            qk[j, p] = sc[0:CHUNK] * decay
            a_mat[j, p] = jnp.where(strict, sc[CHUNK:2 * CHUNK] * beta_p * decay, 0.0)

        x_inv = {u: eye_f - jnp.where(level_masks[0], a_mat[u], 0.0) for u in units}
        for mask in level_masks[1:]:
            y = {u: jnp.dot(jnp.where(mask, a_mat[u], 0.0).astype(BF16), block_diag(x_inv[u]),
                            preferred_element_type=F32) for u in units}
            xy = {u: jnp.dot(x_inv[u].astype(BF16), block_diag(y[u]), preferred_element_type=F32) for u in units}
            x_inv = {u: x_inv[u] - xy[u] for u in units}

        uw = [[None] * nh for _ in members]
        for j, p in units:
            rhs = []
            for h in (2 * p, 2 * p + 1):
                kb_eg = kn[j][h] * (beta[j][h] * lanecol(eg_all[j], h))
                rhs.append(jnp.concatenate([sl(j, 2, h) * beta[j][h], kb_eg], axis=1))
            both = jnp.dot(block_diag(x_inv[j, p]), jnp.concatenate(rhs, axis=0).astype(BF16),
                           preferred_element_type=F32)
            uw[j][2 * p] = both[0:CHUNK]
            uw[j][2 * p + 1] = both[CHUNK:2 * CHUNK]

        for j in members:
            ws = []
            for h in heads:
                qd = qn[j][h] * lanecol(eg_all[j], h)
                ws.append(_bdot(jnp.concatenate([uw[j][h][:, HEAD_DIM:2 * HEAD_DIM], qd], axis=0), s_ref[h]))
            v_new = [uw[j][h][:, 0:HEAD_DIM] - ws[h][0:CHUNK] for h in heads]
            o_intra = [None] * nh
            for p in pairs:
                both = jnp.dot(block_diag(qk[j, p]),
                               jnp.concatenate([v_new[2 * p], v_new[2 * p + 1]], axis=0).astype(BF16),
                               preferred_element_type=F32)
                o_intra[2 * p] = both[0:CHUNK]
                o_intra[2 * p + 1] = both[CHUNK:2 * CHUNK]
            s_add = [_bdot_tn(kn[j][h] * lanecol(tail_all[j], h), v_new[h]) for h in heads]
            for h in heads:
                s_ref[h] = s_ref[h] * lanecol(gl_all[j], h) + s_add[h]
                o = ws[h][CHUNK:2 * CHUNK] + o_intra[h]
                zh = z_ref[rows[j], h * HEAD_DIM:(h + 1) * HEAD_DIM]
                o = o * lax.rsqrt(jnp.mean(o * o, axis=-1, keepdims=True) + NORM_EPS) * normw
                o_ref[rows[j], h * HEAD_DIM:(h + 1) * HEAD_DIM] = (
                    o * (zh * jax.nn.sigmoid(zh))).astype(o_ref.dtype)
        return carry

    lax.fori_loop(0, tb // (group * CHUNK), group_body, 0)


def _gdn(proj, ab, conv_w, alog_lane, dtb_lane, norm_w, *, batch, seq, tb):
    m = proj.shape[0]
    c = DN_HEADS * HEAD_DIM
    nt = seq // tb

    def col(section):
        return lambda b, t: (b * nt + t, section)

    lane_spec = pl.BlockSpec((1, LANES), lambda b, t: (0, 0))
    return pl.pallas_call(
        functools.partial(_gdn_kernel, tb=tb),
        grid=(batch, nt),
        in_specs=[pl.BlockSpec((tb, c), col(0)), pl.BlockSpec((tb, c), col(1)),
                  pl.BlockSpec((tb, c), col(2)), pl.BlockSpec((tb, c), col(3)),
                  pl.BlockSpec((tb, LANES), lambda b, t: (b * nt + t, 0)),
                  pl.BlockSpec((CONV_WIDTH, 3 * c), lambda b, t: (0, 0)),
                  lane_spec, lane_spec, lane_spec],
        out_specs=pl.BlockSpec((tb, c), lambda b, t: (b * nt + t, 0)),
        out_shape=jax.ShapeDtypeStruct((m, c), BF16),
        scratch_shapes=[pltpu.VMEM((8, 3 * c), F32), pltpu.VMEM((DN_HEADS, HEAD_DIM, HEAD_DIM), F32)],
        compiler_params=_params("parallel", "arbitrary"),
        name="gated_delta_rule",
    )(proj, proj, proj, proj, ab, conv_w, alog_lane, dtb_lane, norm_w)


def _swa_kernel(sinks_ref, q_ref, kp_ref, kc_ref, vp_ref, vc_ref, qw_ref, kw_ref, o_ref,
                bias_ref, bias_first_ref, sink_ref, *, nq):
    kvh = pl.program_id(1)
    i = pl.program_id(2)
    blk = WINDOW
    rows = SWA_GROUP * blk

    @pl.when(i == 0)
    def _():
        ri = lax.broadcasted_iota(jnp.int32, (rows, 2 * blk), 0)
        ki = lax.broadcasted_iota(jnp.int32, (rows, 2 * blk), 1)
        dist = (ri & (blk - 1)) + blk - ki
        head = kvh * SWA_GROUP + (ri >> (blk.bit_length() - 1))
        slope = jnp.exp2(-8.0 * (head + 1).astype(F32) / SWA_Q_HEADS)
        bias = jnp.where((dist >= 0) & (dist < WINDOW), -slope * dist.astype(F32), -jnp.inf)
        bias_ref[...] = bias
        bias_first_ref[...] = jnp.where(ki >= blk, bias, -jnp.inf)
        for g in range(SWA_GROUP):
            sink_ref[g * blk:(g + 1) * blk, :] = jnp.full((blk, LANES), sinks_ref[kvh * SWA_GROUP + g], F32)

    @pl.when(i == 1)
    def _():
        bias_first_ref[...] = bias_ref[...]

    def rms(x, w):
        return x * lax.rsqrt(jnp.mean(x * x, axis=-1, keepdims=True) + NORM_EPS) * w

    qw = qw_ref[...] * (HEAD_DIM ** -0.5)
    kw = kw_ref[...]
    subs = range(nq)
    kb = [rms(kp_ref[...], kw)] + [rms(kc_ref[s * blk:(s + 1) * blk, :], kw) for s in subs]
    vb = [vp_ref[...]] + [vc_ref[s * blk:(s + 1) * blk, :] for s in subs]
    q = [jnp.concatenate([rms(q_ref[s * blk:(s + 1) * blk, g * HEAD_DIM:(g + 1) * HEAD_DIM], qw)
                          for g in range(SWA_GROUP)], axis=0) for s in subs]
    sc = [_bdot_nt(q[s], jnp.concatenate([kb[s], kb[s + 1]], axis=0)) for s in subs]
    sink = sink_ref[...]
    p, denom = [], []
    for s in subs:
        x = sc[s] + (bias_first_ref[...] if s == 0 else bias_ref[...])
        mx = jnp.maximum(jnp.max(x, axis=-1, keepdims=True), sink)
        e = jnp.exp(x - jnp.concatenate([mx, mx], axis=1))
        p.append(e)
        denom.append(jnp.sum(e, axis=-1, keepdims=True) + jnp.exp(sink - mx))
    pv = [_bdot(p[s], jnp.concatenate([vb[s], vb[s + 1]], axis=0)) for s in subs]
    for s in subs:
        o = pv[s] / denom[s]
        for g in range(SWA_GROUP):
            o_ref[s * blk:(s + 1) * blk, g * HEAD_DIM:(g + 1) * HEAD_DIM] = (
                o[g * blk:(g + 1) * blk].astype(o_ref.dtype))


def _swa(proj, sinks, q_norm_w, k_norm_w, *, batch, seq, q_col, k_col, v_col, nq):
    m = proj.shape[0]
    blk = WINDOW
    nq = min(nq, seq // blk)
    steps = seq // (nq * blk)
    gw = SWA_GROUP * HEAD_DIM

    def cur(col):
        return lambda b, h, i: (b * steps + i, col + h)

    def prev(col):
        return lambda b, h, i: (b * steps * nq + jnp.maximum(i * nq - 1, 0), col + h)

    wspec = pl.BlockSpec((1, HEAD_DIM), lambda b, h, i: (0, 0))
    table = pltpu.VMEM((SWA_GROUP * blk, 2 * blk), F32)
    return pl.pallas_call(
        functools.partial(_swa_kernel, nq=nq),
        grid=(batch, SWA_KV_HEADS, steps),
        in_specs=[pl.BlockSpec(memory_space=pltpu.SMEM),
                  pl.BlockSpec((nq * blk, gw), lambda b, h, i: (b * steps + i, q_col // SWA_GROUP + h)),
                  pl.BlockSpec((blk, HEAD_DIM), prev(k_col)), pl.BlockSpec((nq * blk, HEAD_DIM), cur(k_col)),
                  pl.BlockSpec((blk, HEAD_DIM), prev(v_col)), pl.BlockSpec((nq * blk, HEAD_DIM), cur(v_col)),
                  wspec, wspec],
        out_specs=pl.BlockSpec((nq * blk, gw), lambda b, h, i: (b * steps + i, h)),
        out_shape=jax.ShapeDtypeStruct((m, SWA_Q_HEADS * HEAD_DIM), BF16),
        scratch_shapes=[table, table, pltpu.VMEM((SWA_GROUP * blk, LANES), F32)],
        compiler_params=_params("parallel", "parallel", "arbitrary"),
        name="sliding_window_gqa",
    )(sinks, proj, proj, proj, proj, proj, q_norm_w, k_norm_w)


def _pad_lanes(v):
    return jnp.pad(v.astype(F32), (0, LANES - v.shape[0])).reshape(1, LANES)


def _layer(x2, batch, seq, attn_norm_w, w_in, conv_w, a_log, dt_bias, dn_norm_w, q_norm_w, k_norm_w,
           sinks, w_out, ffn_norm_w, w_gate, w_up, w_down):
    m, d = x2.shape
    dn_w = DN_HEADS * HEAD_DIM
    q_w = SWA_Q_HEADS * HEAD_DIM
    kv_w = SWA_KV_HEADS * HEAD_DIM
    gates_at = 4 * dn_w
    n_gates = 2 * DN_HEADS

    w_main, w_ab = _split_w_in(w_in.T, gates_at=gates_at, n_gates=n_gates, tn=512)
    f = w_gate.shape[1]
    f_pad = -(-f // 512) * 512
    tm, tn_in, tn_out, tn_gu = min(1024, m), 1024, 512, 512

    in_riders = [_Rider(w_gate, d, f_pad), _Rider(w_out, d, w_out.shape[1])]
    ride_in = all(_rider_block_rows(r, (m // tm) * (w_main.shape[1] // tn_in)) for r in in_riders)
    out_rider = _Rider(w_up, d, f_pad)
    ride_out = _rider_block_rows(out_rider, (m // tm) * (w_out.shape[1] // tn_out)) is not None
    gu_rider = _Rider(w_down, f_pad, d)
    ride_gu = _rider_block_rows(gu_rider, (m // tm) * (f_pad // tn_gu)) is not None

    h = _rmsnorm(x2, attn_norm_w)
    if ride_in:
        proj, wg, wo = _matmul(h, w_main, tm=tm, tn=tn_in, tk=d, out_dtype=F32, riders=in_riders, name="in_proj")
    else:
        proj = _matmul(h, w_main, tm=tm, tn=tn_in, tk=d, out_dtype=F32, name="in_proj")
        wg = _cast_pad_cols(w_gate, f_pad, tr=256, name="cast_w_gate")
        wo = _cast_pad_cols(w_out, w_out.shape[1], tr=512, name="cast_w_out")
    ab = _matmul(h, w_ab, tm=tm, tn=LANES, tk=d, out_dtype=F32, name="gate_proj")

    o_dn = _gdn(proj, ab, conv_w, _pad_lanes(a_log), _pad_lanes(dt_bias), dn_norm_w.reshape(1, HEAD_DIM),
                batch=batch, seq=seq, tb=min(512, seq))
    sec = 4 * dn_w // HEAD_DIM
    o_sw = _swa(proj, sinks.astype(F32), q_norm_w.reshape(1, HEAD_DIM), k_norm_w.reshape(1, HEAD_DIM),
                batch=batch, seq=seq, q_col=sec, k_col=sec + q_w // HEAD_DIM,
                v_col=sec + (q_w + kv_w) // HEAD_DIM, nq=4)

    if ride_out:
        x1, wu = _matmul_pair(o_dn, o_sw, wo, x2, tm=tm, tn=tn_out, riders=[out_rider], name="out_proj")
    else:
        x1 = _matmul_pair(o_dn, o_sw, wo, x2, tm=tm, tn=tn_out, name="out_proj")
        wu = _cast_pad_cols(w_up, f_pad, tr=256, name="cast_w_up")

    h2 = _rmsnorm(x1, ffn_norm_w)
    if ride_gu:
        act, wd = _gate_up(h2, wg, wu, tm=tm, tn=tn_gu, riders=[gu_rider])
    else:
        act = _gate_up(h2, wg, wu, tm=tm, tn=tn_gu)
        wd = _cast_pad_rows(w_down, f_pad, tr=256, name="cast_w_down")
    return _matmul(act, wd, tm=1024, tn=1024, tk=f_pad // 4, out_dtype=F32, residual=x1, name="down_proj")


def kernel(x, attn_norm_w, w_in, conv_w, a_log, dt_bias, dn_norm_w, q_norm_w, k_norm_w, sinks, w_out,
           ffn_norm_w, w_gate, w_up, w_down):
    batch, seq, d = x.shape
    x2 = x.reshape(batch * seq, d)
    for l in range(attn_norm_w.shape[0]):
        x2 = _layer(x2, batch, seq, attn_norm_w[l], w_in[l], conv_w[l], a_log[l], dt_bias[l], dn_norm_w[l],
                    q_norm_w[l], k_norm_w[l], sinks[l], w_out[l], ffn_norm_w[l], w_gate[l], w_up[l], w_down[l])
    return x2.reshape(batch, seq, d)
```

```python
import functools
from typing import NamedTuple

import jax
import jax.numpy as jnp
from jax import lax
from jax.experimental import pallas as pl
from jax.experimental.pallas import tpu as pltpu

NORM_EPS = 1e-6
HEAD_DIM = 128
DN_HEADS = 16
CHUNK = 64
CONV_WIDTH = 4
SWA_Q_HEADS = 16
SWA_KV_HEADS = 4
SWA_GROUP = SWA_Q_HEADS // SWA_KV_HEADS
WINDOW = 128
LANES = 128
VMEM_LIMIT_BYTES = 56 * 1024 * 1024

BF16 = jnp.bfloat16
F32 = jnp.float32


def _params(*semantics):
    return pltpu.CompilerParams(dimension_semantics=semantics, vmem_limit_bytes=VMEM_LIMIT_BYTES)


def _rmsnorm_kernel(x_ref, w_ref, o_ref):
    x = x_ref[...]
    ms = jnp.mean(x * x, axis=-1, keepdims=True)
    o_ref[...] = (x * lax.rsqrt(ms + NORM_EPS) * w_ref[...]).astype(o_ref.dtype)


def _rmsnorm(x, w, tm=256):
    m, d = x.shape
    tm = min(tm, m)
    return pl.pallas_call(
        _rmsnorm_kernel,
        grid=(m // tm,),
        in_specs=[pl.BlockSpec((tm, d), lambda i: (i, 0)), pl.BlockSpec((1, d), lambda i: (0, 0))],
        out_specs=pl.BlockSpec((tm, d), lambda i: (i, 0)),
        out_shape=jax.ShapeDtypeStruct((m, d), BF16),
        compiler_params=_params("parallel"),
        name="rmsnorm",
    )(x, w.reshape(1, d))


def _cast_cols_kernel(w_ref, o_ref):
    n_in = w_ref.shape[1]
    o_ref[:, 0:n_in] = w_ref[...].astype(o_ref.dtype)
    if o_ref.shape[1] > n_in:
        o_ref[:, n_in:] = jnp.zeros((o_ref.shape[0], o_ref.shape[1] - n_in), o_ref.dtype)


def _cast_pad_cols(w, n_out, *, tr, name):
    k, n = w.shape
    return pl.pallas_call(
        _cast_cols_kernel,
        grid=(k // tr,),
        in_specs=[pl.BlockSpec((tr, n), lambda i: (i, 0))],
        out_specs=pl.BlockSpec((tr, n_out), lambda i: (i, 0)),
        out_shape=jax.ShapeDtypeStruct((k, n_out), BF16),
        compiler_params=_params("parallel"),
        name=name,
    )(w)


def _cast_rows_kernel(w_ref, o_ref, *, n_valid):
    i = pl.program_id(0)

    @pl.when(i < n_valid)
    def _():
        o_ref[...] = w_ref[...].astype(o_ref.dtype)

    @pl.when(i >= n_valid)
    def _():
        o_ref[...] = jnp.zeros_like(o_ref)


def _cast_pad_rows(w, k_out, *, tr, name):
    k, n = w.shape
    n_valid = k // tr
    return pl.pallas_call(
        functools.partial(_cast_rows_kernel, n_valid=n_valid),
        grid=(k_out // tr,),
        in_specs=[pl.BlockSpec((tr, n), lambda i: (jnp.minimum(i, n_valid - 1), 0))],
        out_specs=pl.BlockSpec((tr, n), lambda i: (i, 0)),
        out_shape=jax.ShapeDtypeStruct((k_out, n), BF16),
        compiler_params=_params("parallel"),
        name=name,
    )(w)


def _split_main_kernel(a_ref, b_ref, main_ref, *, n_aligned, n_gates):
    j = pl.program_id(0)

    @pl.when(j < n_aligned)
    def _():
        main_ref[...] = a_ref[...].T.astype(BF16)

    @pl.when(j >= n_aligned)
    def _():
        src = jnp.concatenate([a_ref[n_gates:, :], b_ref[...]], axis=0)
        main_ref[...] = src.T.astype(BF16)


def _split_gates_kernel(g_ref, gates_ref, *, n_gates):
    lane = lax.broadcasted_iota(jnp.int32, gates_ref.shape, 1)
    gates_ref[...] = jnp.where(lane < n_gates, g_ref[...].T, 0.0).astype(BF16)


def _split_w_in(w_in_t, *, gates_at, n_gates, tn):
    n, k = w_in_t.shape
    n_main = n - n_gates
    main = pl.pallas_call(
        functools.partial(_split_main_kernel, n_aligned=gates_at // tn, n_gates=n_gates),
        grid=(n_main // tn,),
        in_specs=[pl.BlockSpec((tn, k), lambda j: (j, 0)),
                  pl.BlockSpec((n_gates, k), lambda j: ((j + 1) * (tn // n_gates), 0))],
        out_specs=pl.BlockSpec((k, tn), lambda j: (0, j)),
        out_shape=jax.ShapeDtypeStruct((k, n_main), BF16),
        compiler_params=_params("parallel"),
        name="split_w_in",
    )(w_in_t, w_in_t)
    gates = pl.pallas_call(
        functools.partial(_split_gates_kernel, n_gates=n_gates),
        grid=(1,),
        in_specs=[pl.BlockSpec((LANES, k), lambda j: (gates_at // LANES, 0))],
        out_specs=pl.BlockSpec((k, LANES), lambda j: (0, 0)),
        out_shape=jax.ShapeDtypeStruct((k, LANES), BF16),
        compiler_params=_params("arbitrary"),
        name="split_w_gates",
    )(w_in_t)
    return main, gates


class _Rider(NamedTuple):
    src: jax.Array
    rows_out: int
    cols_out: int


def _rider_block_rows(rider, steps):
    rows = rider.src.shape[0]
    br = 32
    while br <= rows:
        if rows % br == 0 and rider.rows_out % br == 0 and rider.rows_out // br <= steps:
            return br
        br *= 2
    return None


def _rider_plumbing(riders, steps, step_of):
    in_specs, out_specs, out_shapes, meta = [], [], [], []
    for r in riders:
        rows, cols = r.src.shape
        br = _rider_block_rows(r, steps)
        n_valid, n_out = rows // br, r.rows_out // br
        in_specs.append(pl.BlockSpec((br, cols), lambda *g, n=n_valid: (jnp.minimum(step_of(*g), n - 1), 0)))
        out_specs.append(pl.BlockSpec((br, r.cols_out), lambda *g, n=n_out: (jnp.minimum(step_of(*g), n - 1), 0)))
        out_shapes.append(jax.ShapeDtypeStruct((r.rows_out, r.cols_out), BF16))
        meta.append((n_valid, n_out))
    return in_specs, out_specs, out_shapes, tuple(meta)


def _rider_cast(step, src_ref, dst_ref, n_valid, n_out):
    cols = src_ref.shape[1]

    @pl.when(step < n_valid)
    def _():
        dst_ref[:, 0:cols] = src_ref[...].astype(dst_ref.dtype)
        if dst_ref.shape[1] > cols:
            dst_ref[:, cols:] = jnp.zeros((dst_ref.shape[0], dst_ref.shape[1] - cols), dst_ref.dtype)

    if n_out > n_valid:
        @pl.when(jnp.logical_and(step >= n_valid, step < n_out))
        def _():
            dst_ref[...] = jnp.zeros_like(dst_ref)


def _mm_kernel(a_ref, w_ref, *rest, nk, residual, narrow, riders=()):
    nr = len(riders)
    rest = list(rest)
    r_ref = rest.pop(0) if residual else None
    wn_ref = rest.pop(0) if narrow else None
    rider_in = [rest.pop(0) for _ in range(nr)]
    o_ref = rest.pop(0)
    on_ref = rest.pop(0) if narrow else None
    rider_out = rest
    if nr:
        step = pl.program_id(0) * pl.num_programs(1) + pl.program_id(1)
        for src_ref, dst_ref, (n_valid, n_out) in zip(rider_in, rider_out, riders):
            _rider_cast(step, src_ref, dst_ref, n_valid, n_out)
    if narrow:
        @pl.when(pl.program_id(1) == 0)
        def _():
            on_ref[...] = jnp.dot(a_ref[...], wn_ref[...], preferred_element_type=F32)
    part = jnp.dot(a_ref[...], w_ref[...], preferred_element_type=F32)
    if nk == 1:
        if residual:
            part = part + r_ref[...]
        o_ref[...] = part.astype(o_ref.dtype)
        return
    k = pl.program_id(2)

    @pl.when(k == 0)
    def _():
        o_ref[...] = part + r_ref[...] if residual else part

    @pl.when(k > 0)
    def _():
        o_ref[...] += part


def _matmul(a, w, *, tm, tn, tk, out_dtype, residual=None, narrow_w=None, riders=(), name):
    m, kd = a.shape
    n = w.shape[1]
    tm = min(tm, m)
    nk = kd // tk
    nj = n // tn
    assert nk == 1 or (out_dtype == F32 and not riders and narrow_w is None)
    in_specs = [pl.BlockSpec((tm, tk), lambda i, j, k: (i, k)),
                pl.BlockSpec((tk, tn), lambda i, j, k: (k, j))]
    args = [a, w]
    out_specs = [pl.BlockSpec((tm, tn), lambda i, j, k: (i, j))]
    out_shape = [jax.ShapeDtypeStruct((m, n), out_dtype)]
    if residual is not None:
        in_specs.append(pl.BlockSpec((tm, tn), lambda i, j, k: (i, j)))
        args.append(residual)
    if narrow_w is not None:
        in_specs.append(pl.BlockSpec((kd, LANES), lambda i, j, k: (0, 0)))
        args.append(narrow_w)
        out_specs.append(pl.BlockSpec((tm, LANES), lambda i, j, k: (i, 0)))
        out_shape.append(jax.ShapeDtypeStruct((m, LANES), F32))
    r_in, r_out, r_shapes, meta = _rider_plumbing(riders, (m // tm) * nj, lambda i, j, k: i * nj + j)
    outs = pl.pallas_call(
        functools.partial(_mm_kernel, nk=nk, residual=residual is not None, narrow=narrow_w is not None,
                          riders=meta),
        grid=(m // tm, nj, nk),
        in_specs=in_specs + r_in,
        out_specs=out_specs + r_out,
        out_shape=out_shape + r_shapes,
        compiler_params=_params("arbitrary", "arbitrary", "arbitrary"),
        name=name,
    )(*args, *[r.src for r in riders])
    return outs if len(outs) > 1 else outs[0]


def _mm_pair_kernel(a1_ref, a2_ref, w_ref, r_ref, nw_ref, *rest, riders):
    nr = len(riders)
    o_ref, xw_ref, ss_ref = rest[nr:nr + 3]
    if nr:
        step = pl.program_id(0) * pl.num_programs(1) + pl.program_id(1)
        for src_ref, dst_ref, (n_valid, n_out) in zip(rest[:nr], rest[nr + 3:], riders):
            _rider_cast(step, src_ref, dst_ref, n_valid, n_out)
    k1 = a1_ref.shape[1]
    acc = jnp.dot(a1_ref[...], w_ref[0:k1, :], preferred_element_type=F32)
    acc = acc + jnp.dot(a2_ref[...], w_ref[k1:, :], preferred_element_type=F32)
    x = acc + r_ref[...]
    o_ref[...] = x
    xw_ref[...] = (x * nw_ref[...]).astype(xw_ref.dtype)
    sq = x * x
    part = sq[:, 0:LANES]
    for c in range(1, sq.shape[1] // LANES):
        part = part + sq[:, c * LANES:(c + 1) * LANES]

    @pl.when(pl.program_id(1) == 0)
    def _():
        ss_ref[...] = part

    @pl.when(pl.program_id(1) > 0)
    def _():
        ss_ref[...] += part


def _matmul_pair(a1, a2, w, residual, norm_w, *, tm, tn, riders=(), name):
    m, k1 = a1.shape
    k2 = a2.shape[1]
    n = w.shape[1]
    tm = min(tm, m)
    nj = n // tn
    r_in, r_out, r_shapes, meta = _rider_plumbing(riders, (m // tm) * nj, lambda i, j: i * nj + j)
    return pl.pallas_call(
        functools.partial(_mm_pair_kernel, riders=meta),
        grid=(m // tm, nj),
        in_specs=[pl.BlockSpec((tm, k1), lambda i, j: (i, 0)),
                  pl.BlockSpec((tm, k2), lambda i, j: (i, 0)),
                  pl.BlockSpec((k1 + k2, tn), lambda i, j: (0, j)),
                  pl.BlockSpec((tm, tn), lambda i, j: (i, j)),
                  pl.BlockSpec((1, tn), lambda i, j: (0, j))] + r_in,
        out_specs=[pl.BlockSpec((tm, tn), lambda i, j: (i, j)),
                   pl.BlockSpec((tm, tn), lambda i, j: (i, j)),
                   pl.BlockSpec((tm, LANES), lambda i, j: (i, 0))] + r_out,
        out_shape=[jax.ShapeDtypeStruct((m, n), F32), jax.ShapeDtypeStruct((m, n), BF16),
                   jax.ShapeDtypeStruct((m, LANES), F32)] + r_shapes,
        compiler_params=_params("arbitrary", "arbitrary"),
        name=name,
    )(a1, a2, w, residual, norm_w.reshape(1, n), *[r.src for r in riders])


def _gate_up_kernel(a_ref, ss_ref, wg_ref, wu_ref, *rest, riders, width):
    nr = len(riders)
    o_ref = rest[nr]
    rstd_ref = rest[-1]
    if nr:
        step = pl.program_id(0) * pl.num_programs(1) + pl.program_id(1)
        for src_ref, dst_ref, (n_valid, n_out) in zip(rest[:nr], rest[nr + 1:-1], riders):
            _rider_cast(step, src_ref, dst_ref, n_valid, n_out)

    @pl.when(pl.program_id(1) == 0)
    def _():
        ms = jnp.sum(ss_ref[...], axis=-1, keepdims=True) * (1.0 / width)
        rstd_ref[...] = jnp.broadcast_to(lax.rsqrt(ms + NORM_EPS), rstd_ref.shape)

    a = a_ref[...]
    rstd = rstd_ref[...]
    rstd = jnp.concatenate([rstd] * (o_ref.shape[1] // LANES), axis=1)
    g = jnp.dot(a, wg_ref[...], preferred_element_type=F32) * rstd
    u = jnp.dot(a, wu_ref[...], preferred_element_type=F32) * rstd
    o_ref[...] = (g * jax.nn.sigmoid(g) * u).astype(o_ref.dtype)


def _gate_up(a, ss, wg, wu, *, tm, tn, riders=()):
    m, kd = a.shape
    n = wg.shape[1]
    tm = min(tm, m)
    nj = n // tn
    in_specs = [pl.BlockSpec((tm, kd), lambda i, j: (i, 0)),
                pl.BlockSpec((tm, LANES), lambda i, j: (i, 0)),
                pl.BlockSpec((kd, tn), lambda i, j: (0, j)),
                pl.BlockSpec((kd, tn), lambda i, j: (0, j))]
    out_specs = [pl.BlockSpec((tm, tn), lambda i, j: (i, j))]
    out_shape = [jax.ShapeDtypeStruct((m, n), BF16)]
    r_in, r_out, r_shapes, meta = _rider_plumbing(riders, (m // tm) * nj, lambda i, j: i * nj + j)
    outs = pl.pallas_call(
        functools.partial(_gate_up_kernel, riders=meta, width=kd),
        grid=(m // tm, nj),
        in_specs=in_specs + r_in,
        out_specs=out_specs + r_out,
        out_shape=out_shape + r_shapes,
        scratch_shapes=[pltpu.VMEM((tm, LANES), F32)],
        compiler_params=_params("arbitrary", "arbitrary"),
        name="gate_up",
    )(a, ss, wg, wu, *[r.src for r in riders])
    return outs if riders else outs[0]


def _bdot(a, b):
    return jnp.dot(a.astype(BF16), b.astype(BF16), preferred_element_type=F32)


def _bdot_nt(a, b):
    return lax.dot_general(a.astype(BF16), b.astype(BF16), (((1,), (1,)), ((), ())),
                           preferred_element_type=F32)


def _bdot_tn(a, b):
    return lax.dot_general(a.astype(BF16), b.astype(BF16), (((0,), (0,)), ((), ())),
                           preferred_element_type=F32)


def _xdot(a, b):
    return jnp.dot(a, b, preferred_element_type=F32, precision=lax.Precision.HIGHEST)


def _gdn_kernel(q_ref, k_ref, v_ref, z_ref, ab_ref, cw_ref, alog_ref, dtb_ref, normw_ref,
                o_ref, tail_ref, s_ref, *, tb):
    t = pl.program_id(1)
    nh = DN_HEADS
    c = nh * HEAD_DIM
    pad = 8

    @pl.when(t == 0)
    def _():
        tail_ref[...] = jnp.zeros_like(tail_ref)
        s_ref[...] = jnp.zeros_like(s_ref)

    sections = (q_ref, k_ref, v_ref)

    ri = lax.broadcasted_iota(jnp.int32, (CHUNK, LANES), 0)
    lane = lax.broadcasted_iota(jnp.int32, (CHUNK, LANES), 1)
    ci = lane & (CHUNK - 1)
    first = lane < CHUNK
    first2 = lax.broadcasted_iota(jnp.int32, (2 * CHUNK, LANES), 1) < CHUNK
    top2 = lax.broadcasted_iota(jnp.int32, (2 * CHUNK, LANES), 0) < CHUNK
    incl = ri >= ci
    strict = ri > ci
    eye_f = jnp.where(ri == ci, 1.0, 0.0).astype(F32)
    ri_sq = lax.broadcasted_iota(jnp.int32, (CHUNK, CHUNK), 0)
    ci_sq = lax.broadcasted_iota(jnp.int32, (CHUNK, CHUNK), 1)
    tri_ones = jnp.where(ri_sq >= ci_sq, 1.0, 0.0).astype(F32)
    level_masks = []
    for lvl in range(CHUNK.bit_length() - 1):
        same_pair = (ri >> (lvl + 1)) == (ci >> (lvl + 1))
        level_masks.append(same_pair & (((ri >> lvl) & 1) == 1) & (((ci >> lvl) & 1) == 0))
    alog = alog_ref[...]
    dtb = dtb_ref[...]
    normw = normw_ref[...]
    scale = HEAD_DIM ** -0.5
    heads = range(nh)
    pairs = range(nh // 2)
    zeros_hd = jnp.zeros((CHUNK, HEAD_DIM), F32)

    def block_diag(m):
        mb = jnp.concatenate([m, m], axis=0).astype(BF16)
        return jnp.where(first2 == top2, mb, jnp.zeros_like(mb))

    def lanecol(x, h):
        return x[:, h:h + 1]

    ones_hd = jnp.ones((HEAD_DIM, HEAD_DIM), BF16)

    def lane_sum(x):
        return jnp.dot(x.astype(BF16), ones_hd, preferred_element_type=F32)

    group = 2
    members = range(group)

    def group_body(it, carry):
        base = it * (group * CHUNK)
        rows = [pl.ds(pl.multiple_of(base + j * CHUNK, CHUNK), CHUNK) for j in members]

        conv = {}

        def sl(j, section, h):
            if (0, section, h) not in conv:
                lo = section * c + h * HEAD_DIM
                w = cw_ref[:, lo:lo + HEAD_DIM]
                prev = tail_ref[:, lo:lo + HEAD_DIM]
                for m in members:
                    x = sections[section][rows[m], h * HEAD_DIM:(h + 1) * HEAD_DIM]
                    ext = jnp.concatenate([prev, x], axis=0)
                    acc = x * w[CONV_WIDTH - 1:CONV_WIDTH, :]
                    for shift in range(1, CONV_WIDTH):
                        tap = CONV_WIDTH - 1 - shift
                        acc = acc + pltpu.roll(ext, shift, axis=0)[pad:pad + CHUNK, :] * w[tap:tap + 1, :]
                    conv[m, section, h] = acc * jax.nn.sigmoid(acc)
                    prev = x[CHUNK - pad:CHUNK, :]
                tail_ref[:, lo:lo + HEAD_DIM] = prev
            return conv[j, section, h]

        beta_all, gc_all, eg_all, tail_all, gl_all, gc_t = [], [], [], [], [], []
        for j in members:
            ab = ab_ref[rows[j], :]
            g = -jnp.exp(alog) * jax.nn.softplus(ab + dtb)
            beta_all.append(jax.nn.sigmoid(ab))
            gc = _xdot(tri_ones, g)
            gc_last = gc[CHUNK - 1:CHUNK, :]
            gc_all.append(gc)
            eg_all.append(jnp.exp(gc))
            tail_all.append(jnp.exp(gc_last - gc))
            gl_all.append(jnp.exp(gc_last))
            gc_t.append(jnp.concatenate([gc, pltpu.roll(gc, LANES - 1, axis=1)], axis=0).T)

        qn = [[None] * nh for _ in members]
        kn = [[None] * nh for _ in members]
        beta = [[None] * nh for _ in members]
        for j in members:
            for h in heads:
                qh = sl(j, 0, h)
                kh = sl(j, 1, h)
                qn[j][h] = qh * lax.rsqrt(lane_sum(qh * qh) + NORM_EPS) * scale
                kn[j][h] = kh * lax.rsqrt(lane_sum(kh * kh) + NORM_EPS)
                beta[j][h] = lanecol(beta_all[j], DN_HEADS + h)

        units = [(j, p) for j in members for p in pairs]
        qk, a_mat = {}, {}
        for j, p in units:
            h0, h1 = 2 * p, 2 * p + 1
            gc_col = jnp.where(first, lanecol(gc_all[j], h0), lanecol(gc_all[j], h1))
            decay = jnp.exp(jnp.where(incl, gc_col - gc_t[j][h0:h0 + 1, :], -jnp.inf))
            beta_p = jnp.where(first, beta[j][h0], beta[j][h1])
            lhs = jnp.concatenate([jnp.concatenate([qn[j][h0], qn[j][h1]], axis=1),
                                   jnp.concatenate([kn[j][h0], kn[j][h1]], axis=1)], axis=0)
            rhs = jnp.concatenate([jnp.concatenate([kn[j][h0], zeros_hd], axis=1),
                                   jnp.concatenate([zeros_hd, kn[j][h1]], axis=1)], axis=0)
            sc = _bdot_nt(lhs, rhs)
            qk[j, p] = sc[0:CHUNK] * decay
            a_mat[j, p] = jnp.where(strict, sc[CHUNK:2 * CHUNK] * beta_p * decay, 0.0)

        x_inv = {u: eye_f - jnp.where(level_masks[0], a_mat[u], 0.0) for u in units}
        for mask in level_masks[1:]:
            y = {u: jnp.dot(jnp.where(mask, a_mat[u], 0.0).astype(BF16), block_diag(x_inv[u]),
                            preferred_element_type=F32) for u in units}
            xy = {u: jnp.dot(x_inv[u].astype(BF16), block_diag(y[u]), preferred_element_type=F32) for u in units}
            x_inv = {u: x_inv[u] - xy[u] for u in units}

        uw = [[None] * nh for _ in members]
        for j, p in units:
            rhs = []
            for h in (2 * p, 2 * p + 1):
                kb_eg = kn[j][h] * (beta[j][h] * lanecol(eg_all[j], h))
                rhs.append(jnp.concatenate([sl(j, 2, h) * beta[j][h], kb_eg], axis=1))
            both = jnp.dot(block_diag(x_inv[j, p]), jnp.concatenate(rhs, axis=0).astype(BF16),
                           preferred_element_type=F32)
            uw[j][2 * p] = both[0:CHUNK]
            uw[j][2 * p + 1] = both[CHUNK:2 * CHUNK]

        for j in members:
            ws = []
            for h in heads:
                qd = qn[j][h] * lanecol(eg_all[j], h)
                ws.append(_bdot(jnp.concatenate([uw[j][h][:, HEAD_DIM:2 * HEAD_DIM], qd], axis=0), s_ref[h]))
            v_new = [uw[j][h][:, 0:HEAD_DIM] - ws[h][0:CHUNK] for h in heads]
            o_intra = [None] * nh
            for p in pairs:
                both = jnp.dot(block_diag(qk[j, p]),
                               jnp.concatenate([v_new[2 * p], v_new[2 * p + 1]], axis=0).astype(BF16),
                               preferred_element_type=F32)
                o_intra[2 * p] = both[0:CHUNK]
                o_intra[2 * p + 1] = both[CHUNK:2 * CHUNK]
            s_add = [_bdot_tn(kn[j][h] * lanecol(tail_all[j], h), v_new[h]) for h in heads]
            for h in heads:
                s_ref[h] = s_ref[h] * lanecol(gl_all[j], h) + s_add[h]
                o = ws[h][CHUNK:2 * CHUNK] + o_intra[h]
                zh = z_ref[rows[j], h * HEAD_DIM:(h + 1) * HEAD_DIM]
                o = o * lax.rsqrt(jnp.mean(o * o, axis=-1, keepdims=True) + NORM_EPS) * normw
                o_ref[rows[j], h * HEAD_DIM:(h + 1) * HEAD_DIM] = (
                    o * (zh * jax.nn.sigmoid(zh))).astype(o_ref.dtype)
        return carry

    lax.fori_loop(0, tb // (group * CHUNK), group_body, 0)


def _gdn(proj, ab, conv_w, alog_lane, dtb_lane, norm_w, *, batch, seq, tb):
    m = proj.shape[0]
    c = DN_HEADS * HEAD_DIM
    nt = seq // tb

    def col(section):
        return lambda b, t: (b * nt + t, section)

    lane_spec = pl.BlockSpec((1, LANES), lambda b, t: (0, 0))
    return pl.pallas_call(
        functools.partial(_gdn_kernel, tb=tb),
        grid=(batch, nt),
        in_specs=[pl.BlockSpec((tb, c), col(0)), pl.BlockSpec((tb, c), col(1)),
                  pl.BlockSpec((tb, c), col(2)), pl.BlockSpec((tb, c), col(3)),
                  pl.BlockSpec((tb, LANES), lambda b, t: (b * nt + t, 0)),
                  pl.BlockSpec((CONV_WIDTH, 3 * c), lambda b, t: (0, 0)),
                  lane_spec, lane_spec, lane_spec],
        out_specs=pl.BlockSpec((tb, c), lambda b, t: (b * nt + t, 0)),
        out_shape=jax.ShapeDtypeStruct((m, c), BF16),
        scratch_shapes=[pltpu.VMEM((8, 3 * c), F32), pltpu.VMEM((DN_HEADS, HEAD_DIM, HEAD_DIM), F32)],
        compiler_params=_params("parallel", "arbitrary"),
        name="gated_delta_rule",
    )(proj, proj, proj, proj, ab, conv_w, alog_lane, dtb_lane, norm_w)


def _swa_kernel(sinks_ref, q_ref, kp_ref, kc_ref, vp_ref, vc_ref, qw_ref, kw_ref, o_ref,
                bias_ref, bias_first_ref, sink_ref, *, nq):
    kvh = pl.program_id(1)
    i = pl.program_id(2)
    blk = WINDOW
    rows = SWA_GROUP * blk

    @pl.when(i == 0)
    def _():
        ri = lax.broadcasted_iota(jnp.int32, (rows, 2 * blk), 0)
        ki = lax.broadcasted_iota(jnp.int32, (rows, 2 * blk), 1)
        dist = (ri & (blk - 1)) + blk - ki
        head = kvh * SWA_GROUP + (ri >> (blk.bit_length() - 1))
        slope = jnp.exp2(-8.0 * (head + 1).astype(F32) / SWA_Q_HEADS)
        bias = jnp.where((dist >= 0) & (dist < WINDOW), -slope * dist.astype(F32), -jnp.inf)
        bias_ref[...] = bias
        bias_first_ref[...] = jnp.where(ki >= blk, bias, -jnp.inf)
        for g in range(SWA_GROUP):
            sink_ref[g * blk:(g + 1) * blk, :] = jnp.full((blk, LANES), sinks_ref[kvh * SWA_GROUP + g], F32)

    @pl.when(i == 1)
    def _():
        bias_first_ref[...] = bias_ref[...]

    def rms(x, w):
        return x * lax.rsqrt(jnp.mean(x * x, axis=-1, keepdims=True) + NORM_EPS) * w

    qw = qw_ref[...] * (HEAD_DIM ** -0.5)
    kw = kw_ref[...]
    subs = range(nq)
    kb = [rms(kp_ref[...], kw)] + [rms(kc_ref[s * blk:(s + 1) * blk, :], kw) for s in subs]
    vb = [vp_ref[...]] + [vc_ref[s * blk:(s + 1) * blk, :] for s in subs]
    q = [jnp.concatenate([rms(q_ref[s * blk:(s + 1) * blk, g * HEAD_DIM:(g + 1) * HEAD_DIM], qw)
                          for g in range(SWA_GROUP)], axis=0) for s in subs]
    sc = [_bdot_nt(q[s], jnp.concatenate([kb[s], kb[s + 1]], axis=0)) for s in subs]
    sink = sink_ref[...]
    p, denom = [], []
    for s in subs:
        x = sc[s] + (bias_first_ref[...] if s == 0 else bias_ref[...])
        mx = jnp.maximum(jnp.max(x, axis=-1, keepdims=True), sink)
        e = jnp.exp(x - jnp.concatenate([mx, mx], axis=1))
        p.append(e)
        denom.append(jnp.sum(e, axis=-1, keepdims=True) + jnp.exp(sink - mx))
    pv = [_bdot(p[s], jnp.concatenate([vb[s], vb[s + 1]], axis=0)) for s in subs]
    for s in subs:
        o = pv[s] / denom[s]
        for g in range(SWA_GROUP):
            o_ref[s * blk:(s + 1) * blk, g * HEAD_DIM:(g + 1) * HEAD_DIM] = (
                o[g * blk:(g + 1) * blk].astype(o_ref.dtype))


def _swa(proj, sinks, q_norm_w, k_norm_w, *, batch, seq, q_col, k_col, v_col, nq):
    m = proj.shape[0]
    blk = WINDOW
    nq = min(nq, seq // blk)
    steps = seq // (nq * blk)
    gw = SWA_GROUP * HEAD_DIM

    def cur(col):
        return lambda b, h, i: (b * steps + i, col + h)

    def prev(col):
        return lambda b, h, i: (b * steps * nq + jnp.maximum(i * nq - 1, 0), col + h)

    wspec = pl.BlockSpec((1, HEAD_DIM), lambda b, h, i: (0, 0))
    table = pltpu.VMEM((SWA_GROUP * blk, 2 * blk), F32)
    return pl.pallas_call(
        functools.partial(_swa_kernel, nq=nq),
        grid=(batch, SWA_KV_HEADS, steps),
        in_specs=[pl.BlockSpec(memory_space=pltpu.SMEM),
                  pl.BlockSpec((nq * blk, gw), lambda b, h, i: (b * steps + i, q_col // SWA_GROUP + h)),
                  pl.BlockSpec((blk, HEAD_DIM), prev(k_col)), pl.BlockSpec((nq * blk, HEAD_DIM), cur(k_col)),
                  pl.BlockSpec((blk, HEAD_DIM), prev(v_col)), pl.BlockSpec((nq * blk, HEAD_DIM), cur(v_col)),
                  wspec, wspec],
        out_specs=pl.BlockSpec((nq * blk, gw), lambda b, h, i: (b * steps + i, h)),
        out_shape=jax.ShapeDtypeStruct((m, SWA_Q_HEADS * HEAD_DIM), BF16),
        scratch_shapes=[table, table, pltpu.VMEM((SWA_GROUP * blk, LANES), F32)],
        compiler_params=_params("parallel", "parallel", "arbitrary"),
        name="sliding_window_gqa",
    )(sinks, proj, proj, proj, proj, proj, q_norm_w, k_norm_w)


def _pad_lanes(v):
    return jnp.pad(v.astype(F32), (0, LANES - v.shape[0])).reshape(1, LANES)


def _layer(x2, batch, seq, attn_norm_w, w_in, conv_w, a_log, dt_bias, dn_norm_w, q_norm_w, k_norm_w,
           sinks, w_out, ffn_norm_w, w_gate, w_up, w_down):
    m, d = x2.shape
    dn_w = DN_HEADS * HEAD_DIM
    q_w = SWA_Q_HEADS * HEAD_DIM
    kv_w = SWA_KV_HEADS * HEAD_DIM
    gates_at = 4 * dn_w
    n_gates = 2 * DN_HEADS

    w_main, w_ab = _split_w_in(w_in.T, gates_at=gates_at, n_gates=n_gates, tn=512)
    f = w_gate.shape[1]
    f_pad = -(-f // 512) * 512
    tm, tn_in, tn_out, tn_gu = min(1024, m), 1024, 512, 512

    in_riders = [_Rider(w_gate, d, f_pad), _Rider(w_out, d, w_out.shape[1])]
    ride_in = all(_rider_block_rows(r, (m // tm) * (w_main.shape[1] // tn_in)) for r in in_riders)
    out_rider = _Rider(w_up, d, f_pad)
    ride_out = _rider_block_rows(out_rider, (m // tm) * (w_out.shape[1] // tn_out)) is not None
    gu_rider = _Rider(w_down, f_pad, d)
    ride_gu = _rider_block_rows(gu_rider, (m // tm) * (f_pad // tn_gu)) is not None

    h = _rmsnorm(x2, attn_norm_w)
    if ride_in:
        proj, ab, wg, wo = _matmul(h, w_main, tm=tm, tn=tn_in, tk=d, out_dtype=F32, narrow_w=w_ab,
                                   riders=in_riders, name="in_proj")
    else:
        proj, ab = _matmul(h, w_main, tm=tm, tn=tn_in, tk=d, out_dtype=F32, narrow_w=w_ab, name="in_proj")
        wg = _cast_pad_cols(w_gate, f_pad, tr=256, name="cast_w_gate")
        wo = _cast_pad_cols(w_out, w_out.shape[1], tr=512, name="cast_w_out")

    o_dn = _gdn(proj, ab, conv_w, _pad_lanes(a_log), _pad_lanes(dt_bias), dn_norm_w.reshape(1, HEAD_DIM),
                batch=batch, seq=seq, tb=min(512, seq))
    sec = 4 * dn_w // HEAD_DIM
    o_sw = _swa(proj, sinks.astype(F32), q_norm_w.reshape(1, HEAD_DIM), k_norm_w.reshape(1, HEAD_DIM),
                batch=batch, seq=seq, q_col=sec, k_col=sec + q_w // HEAD_DIM,
                v_col=sec + (q_w + kv_w) // HEAD_DIM, nq=4)

    if ride_out:
        x1, xw, ss, wu = _matmul_pair(o_dn, o_sw, wo, x2, ffn_norm_w, tm=tm, tn=tn_out, riders=[out_rider],
                                      name="out_proj")
    else:
        x1, xw, ss = _matmul_pair(o_dn, o_sw, wo, x2, ffn_norm_w, tm=tm, tn=tn_out, name="out_proj")
        wu = _cast_pad_cols(w_up, f_pad, tr=256, name="cast_w_up")

    if ride_gu:
        act, wd = _gate_up(xw, ss, wg, wu, tm=tm, tn=tn_gu, riders=[gu_rider])
    else:
        act = _gate_up(xw, ss, wg, wu, tm=tm, tn=tn_gu)
        wd = _cast_pad_rows(w_down, f_pad, tr=256, name="cast_w_down")
    return _matmul(act, wd, tm=1024, tn=1024, tk=f_pad // 4, out_dtype=F32, residual=x1, name="down_proj")


def kernel(x, attn_norm_w, w_in, conv_w, a_log, dt_bias, dn_norm_w, q_norm_w, k_norm_w, sinks, w_out,
           ffn_norm_w, w_gate, w_up, w_down):
    batch, seq, d = x.shape
    x2 = x.reshape(batch * seq, d)
    for l in range(attn_norm_w.shape[0]):
        x2 = _layer(x2, batch, seq, attn_norm_w[l], w_in[l], conv_w[l], a_log[l], dt_bias[l], dn_norm_w[l],
                    q_norm_w[l], k_norm_w[l], sinks[l], w_out[l], ffn_norm_w[l], w_gate[l], w_up[l], w_down[l])
    return x2.reshape(batch, seq, d)
```

```python
import functools
from typing import NamedTuple

import jax
import jax.numpy as jnp
from jax import lax
from jax.experimental import pallas as pl
from jax.experimental.pallas import tpu as pltpu

NORM_EPS = 1e-6
HEAD_DIM = 128
DN_HEADS = 16
CHUNK = 64
CONV_WIDTH = 4
SWA_Q_HEADS = 16
SWA_KV_HEADS = 4
SWA_GROUP = SWA_Q_HEADS // SWA_KV_HEADS
WINDOW = 128
LANES = 128
VMEM_LIMIT_BYTES = 56 * 1024 * 1024

BF16 = jnp.bfloat16
F32 = jnp.float32


def _params(*semantics):
    return pltpu.CompilerParams(dimension_semantics=semantics, vmem_limit_bytes=VMEM_LIMIT_BYTES)


def _rmsnorm_kernel(x_ref, w_ref, o_ref):
    x = x_ref[...]
    ms = jnp.mean(x * x, axis=-1, keepdims=True)
    o_ref[...] = (x * lax.rsqrt(ms + NORM_EPS) * w_ref[...]).astype(o_ref.dtype)


def _rmsnorm(x, w, tm=256):
    m, d = x.shape
    tm = min(tm, m)
    return pl.pallas_call(
        _rmsnorm_kernel,
        grid=(m // tm,),
        in_specs=[pl.BlockSpec((tm, d), lambda i: (i, 0)), pl.BlockSpec((1, d), lambda i: (0, 0))],
        out_specs=pl.BlockSpec((tm, d), lambda i: (i, 0)),
        out_shape=jax.ShapeDtypeStruct((m, d), BF16),
        compiler_params=_params("parallel"),
        name="rmsnorm",
    )(x, w.reshape(1, d))


def _cast_cols_kernel(w_ref, o_ref):
    n_in = w_ref.shape[1]
    o_ref[:, 0:n_in] = w_ref[...].astype(o_ref.dtype)
    if o_ref.shape[1] > n_in:
        o_ref[:, n_in:] = jnp.zeros((o_ref.shape[0], o_ref.shape[1] - n_in), o_ref.dtype)


def _cast_pad_cols(w, n_out, *, tr, name):
    k, n = w.shape
    return pl.pallas_call(
        _cast_cols_kernel,
        grid=(k // tr,),
        in_specs=[pl.BlockSpec((tr, n), lambda i: (i, 0))],
        out_specs=pl.BlockSpec((tr, n_out), lambda i: (i, 0)),
        out_shape=jax.ShapeDtypeStruct((k, n_out), BF16),
        compiler_params=_params("parallel"),
        name=name,
    )(w)


def _cast_rows_kernel(w_ref, o_ref, *, n_valid):
    i = pl.program_id(0)

    @pl.when(i < n_valid)
    def _():
        o_ref[...] = w_ref[...].astype(o_ref.dtype)

    @pl.when(i >= n_valid)
    def _():
        o_ref[...] = jnp.zeros_like(o_ref)


def _cast_pad_rows(w, k_out, *, tr, name):
    k, n = w.shape
    n_valid = k // tr
    return pl.pallas_call(
        functools.partial(_cast_rows_kernel, n_valid=n_valid),
        grid=(k_out // tr,),
        in_specs=[pl.BlockSpec((tr, n), lambda i: (jnp.minimum(i, n_valid - 1), 0))],
        out_specs=pl.BlockSpec((tr, n), lambda i: (i, 0)),
        out_shape=jax.ShapeDtypeStruct((k_out, n), BF16),
        compiler_params=_params("parallel"),
        name=name,
    )(w)


def _split_main_kernel(a_ref, b_ref, main_ref, *, n_aligned, n_gates):
    j = pl.program_id(0)

    @pl.when(j < n_aligned)
    def _():
        main_ref[...] = a_ref[...].T.astype(BF16)

    @pl.when(j >= n_aligned)
    def _():
        src = jnp.concatenate([a_ref[n_gates:, :], b_ref[...]], axis=0)
        main_ref[...] = src.T.astype(BF16)


def _split_gates_kernel(g_ref, gates_ref, *, n_gates):
    lane = lax.broadcasted_iota(jnp.int32, gates_ref.shape, 1)
    gates_ref[...] = jnp.where(lane < n_gates, g_ref[...].T, 0.0).astype(BF16)


def _split_w_in(w_in_t, *, gates_at, n_gates, tn):
    n, k = w_in_t.shape
    n_main = n - n_gates
    main = pl.pallas_call(
        functools.partial(_split_main_kernel, n_aligned=gates_at // tn, n_gates=n_gates),
        grid=(n_main // tn,),
        in_specs=[pl.BlockSpec((tn, k), lambda j: (j, 0)),
                  pl.BlockSpec((n_gates, k), lambda j: ((j + 1) * (tn // n_gates), 0))],
        out_specs=pl.BlockSpec((k, tn), lambda j: (0, j)),
        out_shape=jax.ShapeDtypeStruct((k, n_main), BF16),
        compiler_params=_params("parallel"),
        name="split_w_in",
    )(w_in_t, w_in_t)
    gates = pl.pallas_call(
        functools.partial(_split_gates_kernel, n_gates=n_gates),
        grid=(1,),
        in_specs=[pl.BlockSpec((LANES, k), lambda j: (gates_at // LANES, 0))],
        out_specs=pl.BlockSpec((k, LANES), lambda j: (0, 0)),
        out_shape=jax.ShapeDtypeStruct((k, LANES), BF16),
        compiler_params=_params("arbitrary"),
        name="split_w_gates",
    )(w_in_t)
    return main, gates


class _Rider(NamedTuple):
    src: jax.Array
    rows_out: int
    cols_out: int


def _rider_block_rows(rider, steps):
    rows = rider.src.shape[0]
    br = 32
    while br <= rows:
        if rows % br == 0 and rider.rows_out % br == 0 and rider.rows_out // br <= steps:
            return br
        br *= 2
    return None


def _rider_plumbing(riders, steps, step_of):
    in_specs, out_specs, out_shapes, meta = [], [], [], []
    for r in riders:
        rows, cols = r.src.shape
        br = _rider_block_rows(r, steps)
        n_valid, n_out = rows // br, r.rows_out // br
        in_specs.append(pl.BlockSpec((br, cols), lambda *g, n=n_valid: (jnp.minimum(step_of(*g), n - 1), 0)))
        out_specs.append(pl.BlockSpec((br, r.cols_out), lambda *g, n=n_out: (jnp.minimum(step_of(*g), n - 1), 0)))
        out_shapes.append(jax.ShapeDtypeStruct((r.rows_out, r.cols_out), BF16))
        meta.append((n_valid, n_out))
    return in_specs, out_specs, out_shapes, tuple(meta)


def _rider_cast(step, src_ref, dst_ref, n_valid, n_out):
    cols = src_ref.shape[1]

    @pl.when(step < n_valid)
    def _():
        dst_ref[:, 0:cols] = src_ref[...].astype(dst_ref.dtype)
        if dst_ref.shape[1] > cols:
            dst_ref[:, cols:] = jnp.zeros((dst_ref.shape[0], dst_ref.shape[1] - cols), dst_ref.dtype)

    if n_out > n_valid:
        @pl.when(jnp.logical_and(step >= n_valid, step < n_out))
        def _():
            dst_ref[...] = jnp.zeros_like(dst_ref)


def _mm_kernel(a_ref, w_ref, *rest, nk, residual, narrow, riders=()):
    nr = len(riders)
    rest = list(rest)
    r_ref = rest.pop(0) if residual else None
    wn_ref = rest.pop(0) if narrow else None
    rider_in = [rest.pop(0) for _ in range(nr)]
    o_ref = rest.pop(0)
    on_ref = rest.pop(0) if narrow else None
    rider_out = rest
    if nr:
        step = pl.program_id(0) * pl.num_programs(1) + pl.program_id(1)
        for src_ref, dst_ref, (n_valid, n_out) in zip(rider_in, rider_out, riders):
            _rider_cast(step, src_ref, dst_ref, n_valid, n_out)
    if narrow:
        @pl.when(pl.program_id(1) == 0)
        def _():
            on_ref[...] = jnp.dot(a_ref[...], wn_ref[...], preferred_element_type=F32)
    part = jnp.dot(a_ref[...], w_ref[...], preferred_element_type=F32)
    if nk == 1:
        if residual:
            part = part + r_ref[...]
        o_ref[...] = part.astype(o_ref.dtype)
        return
    k = pl.program_id(2)

    @pl.when(k == 0)
    def _():
        o_ref[...] = part + r_ref[...] if residual else part

    @pl.when(k > 0)
    def _():
        o_ref[...] += part


def _matmul(a, w, *, tm, tn, tk, out_dtype, residual=None, narrow_w=None, riders=(), name):
    m, kd = a.shape
    n = w.shape[1]
    tm = min(tm, m)
    nk = kd // tk
    nj = n // tn
    assert nk == 1 or (out_dtype == F32 and not riders and narrow_w is None)
    in_specs = [pl.BlockSpec((tm, tk), lambda i, j, k: (i, k)),
                pl.BlockSpec((tk, tn), lambda i, j, k: (k, j))]
    args = [a, w]
    out_specs = [pl.BlockSpec((tm, tn), lambda i, j, k: (i, j))]
    out_shape = [jax.ShapeDtypeStruct((m, n), out_dtype)]
    if residual is not None:
        in_specs.append(pl.BlockSpec((tm, tn), lambda i, j, k: (i, j)))
        args.append(residual)
    if narrow_w is not None:
        in_specs.append(pl.BlockSpec((kd, LANES), lambda i, j, k: (0, 0)))
        args.append(narrow_w)
        out_specs.append(pl.BlockSpec((tm, LANES), lambda i, j, k: (i, 0)))
        out_shape.append(jax.ShapeDtypeStruct((m, LANES), F32))
    r_in, r_out, r_shapes, meta = _rider_plumbing(riders, (m // tm) * nj, lambda i, j, k: i * nj + j)
    outs = pl.pallas_call(
        functools.partial(_mm_kernel, nk=nk, residual=residual is not None, narrow=narrow_w is not None,
                          riders=meta),
        grid=(m // tm, nj, nk),
        in_specs=in_specs + r_in,
        out_specs=out_specs + r_out,
        out_shape=out_shape + r_shapes,
        compiler_params=_params("arbitrary", "arbitrary", "arbitrary"),
        name=name,
    )(*args, *[r.src for r in riders])
    return outs if len(outs) > 1 else outs[0]


def _mm_pair_kernel(a1_ref, a2_ref, w_ref, r_ref, nw_ref, *rest, riders):
    nr = len(riders)
    o_ref, xw_ref, ss_ref = rest[nr:nr + 3]
    if nr:
        step = pl.program_id(0) * pl.num_programs(1) + pl.program_id(1)
        for src_ref, dst_ref, (n_valid, n_out) in zip(rest[:nr], rest[nr + 3:], riders):
            _rider_cast(step, src_ref, dst_ref, n_valid, n_out)
    k1 = a1_ref.shape[1]
    acc = jnp.dot(a1_ref[...], w_ref[0:k1, :], preferred_element_type=F32)
    acc = acc + jnp.dot(a2_ref[...], w_ref[k1:, :], preferred_element_type=F32)
    x = acc + r_ref[...]
    o_ref[...] = x
    xw_ref[...] = (x * nw_ref[...]).astype(xw_ref.dtype)
    sq = x * x
    part = sq[:, 0:LANES]
    for c in range(1, sq.shape[1] // LANES):
        part = part + sq[:, c * LANES:(c + 1) * LANES]

    @pl.when(pl.program_id(1) == 0)
    def _():
        ss_ref[...] = part

    @pl.when(pl.program_id(1) > 0)
    def _():
        ss_ref[...] += part


def _matmul_pair(a1, a2, w, residual, norm_w, *, tm, tn, riders=(), name):
    m, k1 = a1.shape
    k2 = a2.shape[1]
    n = w.shape[1]
    tm = min(tm, m)
    nj = n // tn
    r_in, r_out, r_shapes, meta = _rider_plumbing(riders, (m // tm) * nj, lambda i, j: i * nj + j)
    return pl.pallas_call(
        functools.partial(_mm_pair_kernel, riders=meta),
        grid=(m // tm, nj),
        in_specs=[pl.BlockSpec((tm, k1), lambda i, j: (i, 0)),
                  pl.BlockSpec((tm, k2), lambda i, j: (i, 0)),
                  pl.BlockSpec((k1 + k2, tn), lambda i, j: (0, j)),
                  pl.BlockSpec((tm, tn), lambda i, j: (i, j)),
                  pl.BlockSpec((1, tn), lambda i, j: (0, j))] + r_in,
        out_specs=[pl.BlockSpec((tm, tn), lambda i, j: (i, j)),
                   pl.BlockSpec((tm, tn), lambda i, j: (i, j)),
                   pl.BlockSpec((tm, LANES), lambda i, j: (i, 0))] + r_out,
        out_shape=[jax.ShapeDtypeStruct((m, n), F32), jax.ShapeDtypeStruct((m, n), BF16),
                   jax.ShapeDtypeStruct((m, LANES), F32)] + r_shapes,
        compiler_params=_params("arbitrary", "arbitrary"),
        name=name,
    )(a1, a2, w, residual, norm_w.reshape(1, n), *[r.src for r in riders])


def _gate_up_kernel(a_ref, ss_ref, wg_ref, wu_ref, *rest, riders, width):
    nr = len(riders)
    o_ref = rest[nr]
    rstd_ref = rest[-1]
    if nr:
        step = pl.program_id(0) * pl.num_programs(1) + pl.program_id(1)
        for src_ref, dst_ref, (n_valid, n_out) in zip(rest[:nr], rest[nr + 1:-1], riders):
            _rider_cast(step, src_ref, dst_ref, n_valid, n_out)

    @pl.when(pl.program_id(1) == 0)
    def _():
        ms = jnp.sum(ss_ref[...], axis=-1, keepdims=True) * (1.0 / width)
        rstd_ref[...] = jnp.broadcast_to(lax.rsqrt(ms + NORM_EPS), rstd_ref.shape)

    a = a_ref[...]
    rstd = rstd_ref[...]
    rstd = jnp.concatenate([rstd] * (o_ref.shape[1] // LANES), axis=1)
    g = jnp.dot(a, wg_ref[...], preferred_element_type=F32) * rstd
    u = jnp.dot(a, wu_ref[...], preferred_element_type=F32) * rstd
    o_ref[...] = (g * jax.nn.sigmoid(g) * u).astype(o_ref.dtype)


def _gate_up(a, ss, wg, wu, *, tm, tn, riders=()):
    m, kd = a.shape
    n = wg.shape[1]
    tm = min(tm, m)
    nj = n // tn
    in_specs = [pl.BlockSpec((tm, kd), lambda i, j: (i, 0)),
                pl.BlockSpec((tm, LANES), lambda i, j: (i, 0)),
                pl.BlockSpec((kd, tn), lambda i, j: (0, j)),
                pl.BlockSpec((kd, tn), lambda i, j: (0, j))]
    out_specs = [pl.BlockSpec((tm, tn), lambda i, j: (i, j))]
    out_shape = [jax.ShapeDtypeStruct((m, n), BF16)]
    r_in, r_out, r_shapes, meta = _rider_plumbing(riders, (m // tm) * nj, lambda i, j: i * nj + j)
    outs = pl.pallas_call(
        functools.partial(_gate_up_kernel, riders=meta, width=kd),
        grid=(m // tm, nj),
        in_specs=in_specs + r_in,
        out_specs=out_specs + r_out,
        out_shape=out_shape + r_shapes,
        scratch_shapes=[pltpu.VMEM((tm, LANES), F32)],
        compiler_params=_params("arbitrary", "arbitrary"),
        name="gate_up",
    )(a, ss, wg, wu, *[r.src for r in riders])
    return outs if riders else outs[0]


def _bdot(a, b):
    return jnp.dot(a.astype(BF16), b.astype(BF16), preferred_element_type=F32)


def _bdot_nt(a, b):
    return lax.dot_general(a.astype(BF16), b.astype(BF16), (((1,), (1,)), ((), ())),
                           preferred_element_type=F32)


def _bdot_tn(a, b):
    return lax.dot_general(a.astype(BF16), b.astype(BF16), (((0,), (0,)), ((), ())),
                           preferred_element_type=F32)


def _xdot(a, b):
    return jnp.dot(a, b, preferred_element_type=F32, precision=lax.Precision.HIGHEST)


def _silu(x):
    half = 0.5 * x
    return half + half * jnp.tanh(half)


def _gdn_kernel(q_ref, k_ref, v_ref, z_ref, ab_ref, cw_ref, alog_ref, dtb_ref, normw_ref,
                o_ref, tail_ref, s_ref, *, tb):
    t = pl.program_id(1)
    nh = DN_HEADS
    c = nh * HEAD_DIM
    pad = 8

    @pl.when(t == 0)
    def _():
        tail_ref[...] = jnp.zeros_like(tail_ref)
        s_ref[...] = jnp.zeros_like(s_ref)

    sections = (q_ref, k_ref, v_ref)

    ri = lax.broadcasted_iota(jnp.int32, (CHUNK, LANES), 0)
    lane = lax.broadcasted_iota(jnp.int32, (CHUNK, LANES), 1)
    ci = lane & (CHUNK - 1)
    first = lane < CHUNK
    first2 = lax.broadcasted_iota(jnp.int32, (2 * CHUNK, LANES), 1) < CHUNK
    top2 = lax.broadcasted_iota(jnp.int32, (2 * CHUNK, LANES), 0) < CHUNK
    incl = ri >= ci
    strict = ri > ci
    eye_f = jnp.where(ri == ci, 1.0, 0.0).astype(F32)
    ri_sq = lax.broadcasted_iota(jnp.int32, (CHUNK, CHUNK), 0)
    ci_sq = lax.broadcasted_iota(jnp.int32, (CHUNK, CHUNK), 1)
    tri_ones = jnp.where(ri_sq >= ci_sq, 1.0, 0.0).astype(F32)
    level_masks = []
    for lvl in range(CHUNK.bit_length() - 1):
        same_pair = (ri >> (lvl + 1)) == (ci >> (lvl + 1))
        level_masks.append(same_pair & (((ri >> lvl) & 1) == 1) & (((ci >> lvl) & 1) == 0))
    alog = alog_ref[...]
    dtb = dtb_ref[...]
    normw = normw_ref[...]
    scale = HEAD_DIM ** -0.5
    heads = range(nh)
    pairs = range(nh // 2)
    zeros_hd = jnp.zeros((CHUNK, HEAD_DIM), F32)

    def block_diag(m):
        mb = jnp.concatenate([m, m], axis=0).astype(BF16)
        return jnp.where(first2 == top2, mb, jnp.zeros_like(mb))

    def lanecol(x, h):
        return x[:, h:h + 1]

    ones_hd = jnp.ones((HEAD_DIM, HEAD_DIM), BF16)

    def lane_sum(x):
        return jnp.dot(x.astype(BF16), ones_hd, preferred_element_type=F32)

    group = 2
    members = range(group)

    def group_body(it, carry):
        base = it * (group * CHUNK)
        rows = [pl.ds(pl.multiple_of(base + j * CHUNK, CHUNK), CHUNK) for j in members]

        conv = {}

        def sl(j, section, h):
            if (0, section, h) not in conv:
                lo = section * c + h * HEAD_DIM
                w = cw_ref[:, lo:lo + HEAD_DIM]
                prev = tail_ref[:, lo:lo + HEAD_DIM]
                for m in members:
                    x = sections[section][rows[m], h * HEAD_DIM:(h + 1) * HEAD_DIM]
                    ext = jnp.concatenate([prev, x], axis=0)
                    acc = x * w[CONV_WIDTH - 1:CONV_WIDTH, :]
                    for shift in range(1, CONV_WIDTH):
                        tap = CONV_WIDTH - 1 - shift
                        acc = acc + pltpu.roll(ext, shift, axis=0)[pad:pad + CHUNK, :] * w[tap:tap + 1, :]
                    conv[m, section, h] = _silu(acc)
                    prev = x[CHUNK - pad:CHUNK, :]
                tail_ref[:, lo:lo + HEAD_DIM] = prev
            return conv[j, section, h]

        beta_all, gc_all, eg_all, tail_all, gl_all, gc_t = [], [], [], [], [], []
        for j in members:
            ab = ab_ref[rows[j], :]
            g = -jnp.exp(alog) * jax.nn.softplus(ab + dtb)
            beta_all.append(jax.nn.sigmoid(ab))
            gc = _xdot(tri_ones, g)
            gc_last = gc[CHUNK - 1:CHUNK, :]
            gc_all.append(gc)
            eg_all.append(jnp.exp(gc))
            tail_all.append(jnp.exp(gc_last - gc))
            gl_all.append(jnp.exp(gc_last))
            gc_t.append(jnp.concatenate([gc, pltpu.roll(gc, LANES - 1, axis=1)], axis=0).T)

        qn = [[None] * nh for _ in members]
        kn = [[None] * nh for _ in members]
        beta = [[None] * nh for _ in members]
        for j in members:
            for h in heads:
                qh = sl(j, 0, h)
                kh = sl(j, 1, h)
                qn[j][h] = qh * lax.rsqrt(lane_sum(qh * qh) + NORM_EPS) * scale
                kn[j][h] = kh * lax.rsqrt(lane_sum(kh * kh) + NORM_EPS)
                beta[j][h] = lanecol(beta_all[j], DN_HEADS + h)

        units = [(j, p) for j in members for p in pairs]
        qk, a_mat = {}, {}
        for j, p in units:
            h0, h1 = 2 * p, 2 * p + 1
            gc_col = jnp.where(first, lanecol(gc_all[j], h0), lanecol(gc_all[j], h1))
            decay = jnp.exp(jnp.where(incl, gc_col - gc_t[j][h0:h0 + 1, :], -jnp.inf))
            beta_p = jnp.where(first, beta[j][h0], beta[j][h1])
            lhs = jnp.concatenate([jnp.concatenate([qn[j][h0], qn[j][h1]], axis=1),
                                   jnp.concatenate([kn[j][h0], kn[j][h1]], axis=1)], axis=0)
            rhs = jnp.concatenate([jnp.concatenate([kn[j][h0], zeros_hd], axis=1),
                                   jnp.concatenate([zeros_hd, kn[j][h1]], axis=1)], axis=0)
            sc = _bdot_nt(lhs, rhs)
            qk[j, p] = sc[0:CHUNK] * decay
            a_mat[j, p] = jnp.where(strict, sc[CHUNK:2 * CHUNK] * beta_p * decay, 0.0)

        x_inv = {u: eye_f - jnp.where(level_masks[0], a_mat[u], 0.0) for u in units}
        for mask in level_masks[1:]:
            y = {u: jnp.dot(jnp.where(mask, a_mat[u], 0.0).astype(BF16), block_diag(x_inv[u]),
                            preferred_element_type=F32) for u in units}
            xy = {u: jnp.dot(x_inv[u].astype(BF16), block_diag(y[u]), preferred_element_type=F32) for u in units}
            x_inv = {u: x_inv[u] - xy[u] for u in units}

        uw = [[None] * nh for _ in members]
        for j, p in units:
            rhs = []
            for h in (2 * p, 2 * p + 1):
                kb_eg = kn[j][h] * (beta[j][h] * lanecol(eg_all[j], h))
                rhs.append(jnp.concatenate([sl(j, 2, h) * beta[j][h], kb_eg], axis=1))
            both = jnp.dot(block_diag(x_inv[j, p]), jnp.concatenate(rhs, axis=0).astype(BF16),
                           preferred_element_type=F32)
            uw[j][2 * p] = both[0:CHUNK]
            uw[j][2 * p + 1] = both[CHUNK:2 * CHUNK]

        for j in members:
            ws = []
            for h in heads:
                qd = qn[j][h] * lanecol(eg_all[j], h)
                ws.append(_bdot(jnp.concatenate([uw[j][h][:, HEAD_DIM:2 * HEAD_DIM], qd], axis=0), s_ref[h]))
            v_new = [uw[j][h][:, 0:HEAD_DIM] - ws[h][0:CHUNK] for h in heads]
            o_intra = [None] * nh
            for p in pairs:
                both = jnp.dot(block_diag(qk[j, p]),
                               jnp.concatenate([v_new[2 * p], v_new[2 * p + 1]], axis=0).astype(BF16),
                               preferred_element_type=F32)
                o_intra[2 * p] = both[0:CHUNK]
                o_intra[2 * p + 1] = both[CHUNK:2 * CHUNK]
            s_add = [_bdot_tn(kn[j][h] * lanecol(tail_all[j], h), v_new[h]) for h in heads]
            for h in heads:
                s_ref[h] = s_ref[h] * lanecol(gl_all[j], h) + s_add[h]
                o = ws[h][CHUNK:2 * CHUNK] + o_intra[h]
                zh = z_ref[rows[j], h * HEAD_DIM:(h + 1) * HEAD_DIM]
                o = o * lax.rsqrt(jnp.mean(o * o, axis=-1, keepdims=True) + NORM_EPS) * normw
                o_ref[rows[j], h * HEAD_DIM:(h + 1) * HEAD_DIM] = (
                    o * _silu(zh)).astype(o_ref.dtype)
        return carry

    lax.fori_loop(0, tb // (group * CHUNK), group_body, 0)


def _gdn(proj, ab, conv_w, alog_lane, dtb_lane, norm_w, *, batch, seq, tb):
    m = proj.shape[0]
    c = DN_HEADS * HEAD_DIM
    nt = seq // tb

    def col(section):
        return lambda b, t: (b * nt + t, section)

    lane_spec = pl.BlockSpec((1, LANES), lambda b, t: (0, 0))
    return pl.pallas_call(
        functools.partial(_gdn_kernel, tb=tb),
        grid=(batch, nt),
        in_specs=[pl.BlockSpec((tb, c), col(0)), pl.BlockSpec((tb, c), col(1)),
                  pl.BlockSpec((tb, c), col(2)), pl.BlockSpec((tb, c), col(3)),
                  pl.BlockSpec((tb, LANES), lambda b, t: (b * nt + t, 0)),
                  pl.BlockSpec((CONV_WIDTH, 3 * c), lambda b, t: (0, 0)),
                  lane_spec, lane_spec, lane_spec],
        out_specs=pl.BlockSpec((tb, c), lambda b, t: (b * nt + t, 0)),
        out_shape=jax.ShapeDtypeStruct((m, c), BF16),
        scratch_shapes=[pltpu.VMEM((8, 3 * c), F32), pltpu.VMEM((DN_HEADS, HEAD_DIM, HEAD_DIM), F32)],
        compiler_params=_params("parallel", "arbitrary"),
        name="gated_delta_rule",
    )(proj, proj, proj, proj, ab, conv_w, alog_lane, dtb_lane, norm_w)


def _swa_kernel(sinks_ref, q_ref, kp_ref, kc_ref, vp_ref, vc_ref, qw_ref, kw_ref, o_ref,
                bias_ref, bias_first_ref, sink_ref, *, nq):
    kvh = pl.program_id(1)
    i = pl.program_id(2)
    blk = WINDOW
    rows = SWA_GROUP * blk

    @pl.when(i == 0)
    def _():
        ri = lax.broadcasted_iota(jnp.int32, (rows, 2 * blk), 0)
        ki = lax.broadcasted_iota(jnp.int32, (rows, 2 * blk), 1)
        dist = (ri & (blk - 1)) + blk - ki
        head = kvh * SWA_GROUP + (ri >> (blk.bit_length() - 1))
        slope = jnp.exp2(-8.0 * (head + 1).astype(F32) / SWA_Q_HEADS)
        bias = jnp.where((dist >= 0) & (dist < WINDOW), -slope * dist.astype(F32), -jnp.inf)
        bias_ref[...] = bias
        bias_first_ref[...] = jnp.where(ki >= blk, bias, -jnp.inf)
        for g in range(SWA_GROUP):
            sink_ref[g * blk:(g + 1) * blk, :] = jnp.full((blk, LANES), sinks_ref[kvh * SWA_GROUP + g], F32)

    @pl.when(i == 1)
    def _():
        bias_first_ref[...] = bias_ref[...]

    def rms(x, w):
        return x * lax.rsqrt(jnp.mean(x * x, axis=-1, keepdims=True) + NORM_EPS) * w

    qw = qw_ref[...] * (HEAD_DIM ** -0.5)
    kw = kw_ref[...]
    subs = range(nq)
    kb = [rms(kp_ref[...], kw)] + [rms(kc_ref[s * blk:(s + 1) * blk, :], kw) for s in subs]
    vb = [vp_ref[...]] + [vc_ref[s * blk:(s + 1) * blk, :] for s in subs]
    q = [jnp.concatenate([rms(q_ref[s * blk:(s + 1) * blk, g * HEAD_DIM:(g + 1) * HEAD_DIM], qw)
                          for g in range(SWA_GROUP)], axis=0) for s in subs]
    sc = [_bdot_nt(q[s], jnp.concatenate([kb[s], kb[s + 1]], axis=0)) for s in subs]
    sink = sink_ref[...]
    p, denom = [], []
    for s in subs:
        x = sc[s] + (bias_first_ref[...] if s == 0 else bias_ref[...])
        mx = jnp.maximum(jnp.max(x, axis=-1, keepdims=True), sink)
        e = jnp.exp(x - jnp.concatenate([mx, mx], axis=1))
        p.append(e)
        denom.append(jnp.sum(e, axis=-1, keepdims=True) + jnp.exp(sink - mx))
    pv = [_bdot(p[s], jnp.concatenate([vb[s], vb[s + 1]], axis=0)) for s in subs]
    for s in subs:
        o = pv[s] / denom[s]
        for g in range(SWA_GROUP):
            o_ref[s * blk:(s + 1) * blk, g * HEAD_DIM:(g + 1) * HEAD_DIM] = (
                o[g * blk:(g + 1) * blk].astype(o_ref.dtype))


def _swa(proj, sinks, q_norm_w, k_norm_w, *, batch, seq, q_col, k_col, v_col, nq):
    m = proj.shape[0]
    blk = WINDOW
    nq = min(nq, seq // blk)
    steps = seq // (nq * blk)
    gw = SWA_GROUP * HEAD_DIM

    def cur(col):
        return lambda b, h, i: (b * steps + i, col + h)

    def prev(col):
        return lambda b, h, i: (b * steps * nq + jnp.maximum(i * nq - 1, 0), col + h)

    wspec = pl.BlockSpec((1, HEAD_DIM), lambda b, h, i: (0, 0))
    table = pltpu.VMEM((SWA_GROUP * blk, 2 * blk), F32)
    return pl.pallas_call(
        functools.partial(_swa_kernel, nq=nq),
        grid=(batch, SWA_KV_HEADS, steps),
        in_specs=[pl.BlockSpec(memory_space=pltpu.SMEM),
                  pl.BlockSpec((nq * blk, gw), lambda b, h, i: (b * steps + i, q_col // SWA_GROUP + h)),
                  pl.BlockSpec((blk, HEAD_DIM), prev(k_col)), pl.BlockSpec((nq * blk, HEAD_DIM), cur(k_col)),
                  pl.BlockSpec((blk, HEAD_DIM), prev(v_col)), pl.BlockSpec((nq * blk, HEAD_DIM), cur(v_col)),
                  wspec, wspec],
        out_specs=pl.BlockSpec((nq * blk, gw), lambda b, h, i: (b * steps + i, h)),
        out_shape=jax.ShapeDtypeStruct((m, SWA_Q_HEADS * HEAD_DIM), BF16),
        scratch_shapes=[table, table, pltpu.VMEM((SWA_GROUP * blk, LANES), F32)],
        compiler_params=_params("parallel", "parallel", "arbitrary"),
        name="sliding_window_gqa",
    )(sinks, proj, proj, proj, proj, proj, q_norm_w, k_norm_w)


def _pad_lanes(v):
    return jnp.pad(v.astype(F32), (0, LANES - v.shape[0])).reshape(1, LANES)


def _layer(x2, batch, seq, attn_norm_w, w_in, conv_w, a_log, dt_bias, dn_norm_w, q_norm_w, k_norm_w,
           sinks, w_out, ffn_norm_w, w_gate, w_up, w_down):
    m, d = x2.shape
    dn_w = DN_HEADS * HEAD_DIM
    q_w = SWA_Q_HEADS * HEAD_DIM
    kv_w = SWA_KV_HEADS * HEAD_DIM
    gates_at = 4 * dn_w
    n_gates = 2 * DN_HEADS

    w_main, w_ab = _split_w_in(w_in.T, gates_at=gates_at, n_gates=n_gates, tn=512)
    f = w_gate.shape[1]
    f_pad = -(-f // 512) * 512
    tm, tn_in, tn_out, tn_gu = min(1024, m), 1024, 512, 512

    in_riders = [_Rider(w_gate, d, f_pad), _Rider(w_out, d, w_out.shape[1])]
    ride_in = all(_rider_block_rows(r, (m // tm) * (w_main.shape[1] // tn_in)) for r in in_riders)
    out_rider = _Rider(w_up, d, f_pad)
    ride_out = _rider_block_rows(out_rider, (m // tm) * (w_out.shape[1] // tn_out)) is not None
    gu_rider = _Rider(w_down, f_pad, d)
    ride_gu = _rider_block_rows(gu_rider, (m // tm) * (f_pad // tn_gu)) is not None

    h = _rmsnorm(x2, attn_norm_w)
    if ride_in:
        proj, ab, wg, wo = _matmul(h, w_main, tm=tm, tn=tn_in, tk=d, out_dtype=F32, narrow_w=w_ab,
                                   riders=in_riders, name="in_proj")
    else:
        proj, ab = _matmul(h, w_main, tm=tm, tn=tn_in, tk=d, out_dtype=F32, narrow_w=w_ab, name="in_proj")
        wg = _cast_pad_cols(w_gate, f_pad, tr=256, name="cast_w_gate")
        wo = _cast_pad_cols(w_out, w_out.shape[1], tr=512, name="cast_w_out")

    o_dn = _gdn(proj, ab, conv_w, _pad_lanes(a_log), _pad_lanes(dt_bias), dn_norm_w.reshape(1, HEAD_DIM),
                batch=batch, seq=seq, tb=min(512, seq))
    sec = 4 * dn_w // HEAD_DIM
    o_sw = _swa(proj, sinks.astype(F32), q_norm_w.reshape(1, HEAD_DIM), k_norm_w.reshape(1, HEAD_DIM),
                batch=batch, seq=seq, q_col=sec, k_col=sec + q_w // HEAD_DIM,
                v_col=sec + (q_w + kv_w) // HEAD_DIM, nq=8)

    if ride_out:
        x1, xw, ss, wu = _matmul_pair(o_dn, o_sw, wo, x2, ffn_norm_w, tm=tm, tn=tn_out, riders=[out_rider],
                                      name="out_proj")
    else:
        x1, xw, ss = _matmul_pair(o_dn, o_sw, wo, x2, ffn_norm_w, tm=tm, tn=tn_out, name="out_proj")
        wu = _cast_pad_cols(w_up, f_pad, tr=256, name="cast_w_up")

    if ride_gu:
        act, wd = _gate_up(xw, ss, wg, wu, tm=tm, tn=tn_gu, riders=[gu_rider])
    else:
        act = _gate_up(xw, ss, wg, wu, tm=tm, tn=tn_gu)
        wd = _cast_pad_rows(w_down, f_pad, tr=256, name="cast_w_down")
    return _matmul(act, wd, tm=1024, tn=1024, tk=f_pad // 4, out_dtype=F32, residual=x1, name="down_proj")


def kernel(x, attn_norm_w, w_in, conv_w, a_log, dt_bias, dn_norm_w, q_norm_w, k_norm_w, sinks, w_out,
           ffn_norm_w, w_gate, w_up, w_down):
    batch, seq, d = x.shape
    x2 = x.reshape(batch * seq, d)
    for l in range(attn_norm_w.shape[0]):
        x2 = _layer(x2, batch, seq, attn_norm_w[l], w_in[l], conv_w[l], a_log[l], dt_bias[l], dn_norm_w[l],
                    q_norm_w[l], k_norm_w[l], sinks[l], w_out[l], ffn_norm_w[l], w_gate[l], w_up[l], w_down[l])
    return x2.reshape(batch, seq, d)
```

```python
import functools
from typing import NamedTuple

import jax
import jax.numpy as jnp
from jax import lax
from jax.experimental import pallas as pl
from jax.experimental.pallas import tpu as pltpu

NORM_EPS = 1e-6
HEAD_DIM = 128
DN_HEADS = 16
CHUNK = 64
CONV_WIDTH = 4
SWA_Q_HEADS = 16
SWA_KV_HEADS = 4
SWA_GROUP = SWA_Q_HEADS // SWA_KV_HEADS
WINDOW = 128
LANES = 128
VMEM_LIMIT_BYTES = 56 * 1024 * 1024

BF16 = jnp.bfloat16
F32 = jnp.float32


def _params(*semantics):
    return pltpu.CompilerParams(dimension_semantics=semantics, vmem_limit_bytes=VMEM_LIMIT_BYTES)


def _rmsnorm_kernel(x_ref, w_ref, o_ref):
    x = x_ref[...]
    ms = jnp.mean(x * x, axis=-1, keepdims=True)
    o_ref[...] = (x * lax.rsqrt(ms + NORM_EPS) * w_ref[...]).astype(o_ref.dtype)


def _rmsnorm(x, w, tm=256):
    m, d = x.shape
    tm = min(tm, m)
    return pl.pallas_call(
        _rmsnorm_kernel,
        grid=(m // tm,),
        in_specs=[pl.BlockSpec((tm, d), lambda i: (i, 0)), pl.BlockSpec((1, d), lambda i: (0, 0))],
        out_specs=pl.BlockSpec((tm, d), lambda i: (i, 0)),
        out_shape=jax.ShapeDtypeStruct((m, d), BF16),
        compiler_params=_params("parallel"),
        name="rmsnorm",
    )(x, w.reshape(1, d))


def _cast_cols_kernel(w_ref, o_ref):
    n_in = w_ref.shape[1]
    o_ref[:, 0:n_in] = w_ref[...].astype(o_ref.dtype)
    if o_ref.shape[1] > n_in:
        o_ref[:, n_in:] = jnp.zeros((o_ref.shape[0], o_ref.shape[1] - n_in), o_ref.dtype)


def _cast_pad_cols(w, n_out, *, tr, name):
    k, n = w.shape
    return pl.pallas_call(
        _cast_cols_kernel,
        grid=(k // tr,),
        in_specs=[pl.BlockSpec((tr, n), lambda i: (i, 0))],
        out_specs=pl.BlockSpec((tr, n_out), lambda i: (i, 0)),
        out_shape=jax.ShapeDtypeStruct((k, n_out), BF16),
        compiler_params=_params("parallel"),
        name=name,
    )(w)


def _cast_rows_kernel(w_ref, o_ref, *, n_valid):
    i = pl.program_id(0)

    @pl.when(i < n_valid)
    def _():
        o_ref[...] = w_ref[...].astype(o_ref.dtype)

    @pl.when(i >= n_valid)
    def _():
        o_ref[...] = jnp.zeros_like(o_ref)


def _cast_pad_rows(w, k_out, *, tr, name):
    k, n = w.shape
    n_valid = k // tr
    return pl.pallas_call(
        functools.partial(_cast_rows_kernel, n_valid=n_valid),
        grid=(k_out // tr,),
        in_specs=[pl.BlockSpec((tr, n), lambda i: (jnp.minimum(i, n_valid - 1), 0))],
        out_specs=pl.BlockSpec((tr, n), lambda i: (i, 0)),
        out_shape=jax.ShapeDtypeStruct((k_out, n), BF16),
        compiler_params=_params("parallel"),
        name=name,
    )(w)


def _split_main_kernel(a_ref, b_ref, main_ref, *, n_aligned, n_gates):
    j = pl.program_id(0)

    @pl.when(j < n_aligned)
    def _():
        main_ref[...] = a_ref[...].T.astype(BF16)

    @pl.when(j >= n_aligned)
    def _():
        src = jnp.concatenate([a_ref[n_gates:, :], b_ref[...]], axis=0)
        main_ref[...] = src.T.astype(BF16)


def _split_gates_kernel(g_ref, gates_ref, *, n_gates):
    lane = lax.broadcasted_iota(jnp.int32, gates_ref.shape, 1)
    gates_ref[...] = jnp.where(lane < n_gates, g_ref[...].T, 0.0).astype(BF16)


def _split_w_in(w_in_t, *, gates_at, n_gates, tn):
    n, k = w_in_t.shape
    n_main = n - n_gates
    main = pl.pallas_call(
        functools.partial(_split_main_kernel, n_aligned=gates_at // tn, n_gates=n_gates),
        grid=(n_main // tn,),
        in_specs=[pl.BlockSpec((tn, k), lambda j: (j, 0)),
                  pl.BlockSpec((n_gates, k), lambda j: ((j + 1) * (tn // n_gates), 0))],
        out_specs=pl.BlockSpec((k, tn), lambda j: (0, j)),
        out_shape=jax.ShapeDtypeStruct((k, n_main), BF16),
        compiler_params=_params("parallel"),
        name="split_w_in",
    )(w_in_t, w_in_t)
    gates = pl.pallas_call(
        functools.partial(_split_gates_kernel, n_gates=n_gates),
        grid=(1,),
        in_specs=[pl.BlockSpec((LANES, k), lambda j: (gates_at // LANES, 0))],
        out_specs=pl.BlockSpec((k, LANES), lambda j: (0, 0)),
        out_shape=jax.ShapeDtypeStruct((k, LANES), BF16),
        compiler_params=_params("arbitrary"),
        name="split_w_gates",
    )(w_in_t)
    return main, gates


class _Rider(NamedTuple):
    src: jax.Array
    rows_out: int
    cols_out: int


def _rider_block_rows(rider, steps):
    rows = rider.src.shape[0]
    br = 32
    while br <= rows:
        if rows % br == 0 and rider.rows_out % br == 0 and rider.rows_out // br <= steps:
            return br
        br *= 2
    return None


def _rider_plumbing(riders, steps, step_of):
    in_specs, out_specs, out_shapes, meta = [], [], [], []
    for r in riders:
        rows, cols = r.src.shape
        br = _rider_block_rows(r, steps)
        n_valid, n_out = rows // br, r.rows_out // br
        in_specs.append(pl.BlockSpec((br, cols), lambda *g, n=n_valid: (jnp.minimum(step_of(*g), n - 1), 0)))
        out_specs.append(pl.BlockSpec((br, r.cols_out), lambda *g, n=n_out: (jnp.minimum(step_of(*g), n - 1), 0)))
        out_shapes.append(jax.ShapeDtypeStruct((r.rows_out, r.cols_out), BF16))
        meta.append((n_valid, n_out))
    return in_specs, out_specs, out_shapes, tuple(meta)


def _rider_cast(step, src_ref, dst_ref, n_valid, n_out):
    cols = src_ref.shape[1]

    @pl.when(step < n_valid)
    def _():
        dst_ref[:, 0:cols] = src_ref[...].astype(dst_ref.dtype)
        if dst_ref.shape[1] > cols:
            dst_ref[:, cols:] = jnp.zeros((dst_ref.shape[0], dst_ref.shape[1] - cols), dst_ref.dtype)

    if n_out > n_valid:
        @pl.when(jnp.logical_and(step >= n_valid, step < n_out))
        def _():
            dst_ref[...] = jnp.zeros_like(dst_ref)


def _mm_kernel(a_ref, w_ref, *rest, nk, residual, narrow, riders=()):
    nr = len(riders)
    rest = list(rest)
    r_ref = rest.pop(0) if residual else None
    wn_ref = rest.pop(0) if narrow else None
    rider_in = [rest.pop(0) for _ in range(nr)]
    o_ref = rest.pop(0)
    on_ref = rest.pop(0) if narrow else None
    rider_out = rest
    if nr:
        step = pl.program_id(0) * pl.num_programs(1) + pl.program_id(1)
        for src_ref, dst_ref, (n_valid, n_out) in zip(rider_in, rider_out, riders):
            _rider_cast(step, src_ref, dst_ref, n_valid, n_out)
    if narrow:
        @pl.when(pl.program_id(1) == 0)
        def _():
            on_ref[...] = jnp.dot(a_ref[...], wn_ref[...], preferred_element_type=F32)
    if nk == 1:
        part = jnp.dot(a_ref[...], w_ref[...], preferred_element_type=F32)
        if residual:
            part = part + r_ref[...]
        o_ref[...] = part.astype(o_ref.dtype)
        return
    k = pl.program_id(2)

    @pl.when(k == 0)
    def _():
        if residual:
            o_ref[...] = r_ref[...]
        else:
            o_ref[...] = jnp.zeros_like(o_ref)

    o_ref[...] += jnp.dot(a_ref[...], w_ref[...], preferred_element_type=F32)


def _matmul(a, w, *, tm, tn, tk, out_dtype, residual=None, narrow_w=None, riders=(), name):
    m, kd = a.shape
    n = w.shape[1]
    tm = min(tm, m)
    nk = kd // tk
    nj = n // tn
    assert nk == 1 or (out_dtype == F32 and not riders and narrow_w is None)
    in_specs = [pl.BlockSpec((tm, tk), lambda i, j, k: (i, k)),
                pl.BlockSpec((tk, tn), lambda i, j, k: (k, j))]
    args = [a, w]
    out_specs = [pl.BlockSpec((tm, tn), lambda i, j, k: (i, j))]
    out_shape = [jax.ShapeDtypeStruct((m, n), out_dtype)]
    if residual is not None:
        in_specs.append(pl.BlockSpec((tm, tn), lambda i, j, k: (i, j)))
        args.append(residual)
    if narrow_w is not None:
        in_specs.append(pl.BlockSpec((kd, LANES), lambda i, j, k: (0, 0)))
        args.append(narrow_w)
        out_specs.append(pl.BlockSpec((tm, LANES), lambda i, j, k: (i, 0)))
        out_shape.append(jax.ShapeDtypeStruct((m, LANES), F32))
    r_in, r_out, r_shapes, meta = _rider_plumbing(riders, (m // tm) * nj, lambda i, j, k: i * nj + j)
    outs = pl.pallas_call(
        functools.partial(_mm_kernel, nk=nk, residual=residual is not None, narrow=narrow_w is not None,
                          riders=meta),
        grid=(m // tm, nj, nk),
        in_specs=in_specs + r_in,
        out_specs=out_specs + r_out,
        out_shape=out_shape + r_shapes,
        compiler_params=_params("arbitrary", "arbitrary", "arbitrary"),
        name=name,
    )(*args, *[r.src for r in riders])
    return outs if len(outs) > 1 else outs[0]


def _mm_pair_kernel(a1_ref, a2_ref, w_ref, r_ref, nw_ref, *rest, riders):
    nr = len(riders)
    o_ref, xw_ref, ss_ref = rest[nr:nr + 3]
    if nr:
        step = pl.program_id(0) * pl.num_programs(1) + pl.program_id(1)
        for src_ref, dst_ref, (n_valid, n_out) in zip(rest[:nr], rest[nr + 3:], riders):
            _rider_cast(step, src_ref, dst_ref, n_valid, n_out)
    k1 = a1_ref.shape[1]
    acc = jnp.dot(a1_ref[...], w_ref[0:k1, :], preferred_element_type=F32)
    acc = acc + jnp.dot(a2_ref[...], w_ref[k1:, :], preferred_element_type=F32)
    x = acc + r_ref[...]
    o_ref[...] = x
    xw_ref[...] = (x * nw_ref[...]).astype(xw_ref.dtype)
    sq = x * x
    part = sq[:, 0:LANES]
    for c in range(1, sq.shape[1] // LANES):
        part = part + sq[:, c * LANES:(c + 1) * LANES]

    @pl.when(pl.program_id(1) == 0)
    def _():
        ss_ref[...] = part

    @pl.when(pl.program_id(1) > 0)
    def _():
        ss_ref[...] += part


def _matmul_pair(a1, a2, w, residual, norm_w, *, tm, tn, riders=(), name):
    m, k1 = a1.shape
    k2 = a2.shape[1]
    n = w.shape[1]
    tm = min(tm, m)
    nj = n // tn
    r_in, r_out, r_shapes, meta = _rider_plumbing(riders, (m // tm) * nj, lambda i, j: i * nj + j)
    return pl.pallas_call(
        functools.partial(_mm_pair_kernel, riders=meta),
        grid=(m // tm, nj),
        in_specs=[pl.BlockSpec((tm, k1), lambda i, j: (i, 0)),
                  pl.BlockSpec((tm, k2), lambda i, j: (i, 0)),
                  pl.BlockSpec((k1 + k2, tn), lambda i, j: (0, j)),
                  pl.BlockSpec((tm, tn), lambda i, j: (i, j)),
                  pl.BlockSpec((1, tn), lambda i, j: (0, j))] + r_in,
        out_specs=[pl.BlockSpec((tm, tn), lambda i, j: (i, j)),
                   pl.BlockSpec((tm, tn), lambda i, j: (i, j)),
                   pl.BlockSpec((tm, LANES), lambda i, j: (i, 0))] + r_out,
        out_shape=[jax.ShapeDtypeStruct((m, n), F32), jax.ShapeDtypeStruct((m, n), BF16),
                   jax.ShapeDtypeStruct((m, LANES), F32)] + r_shapes,
        compiler_params=_params("arbitrary", "arbitrary"),
        name=name,
    )(a1, a2, w, residual, norm_w.reshape(1, n), *[r.src for r in riders])


def _gate_up_kernel(a_ref, ss_ref, wg_ref, wu_ref, *rest, riders, width):
    nr = len(riders)
    o_ref = rest[nr]
    rstd_ref = rest[-1]
    if nr:
        step = pl.program_id(0) * pl.num_programs(1) + pl.program_id(1)
        for src_ref, dst_ref, (n_valid, n_out) in zip(rest[:nr], rest[nr + 1:-1], riders):
            _rider_cast(step, src_ref, dst_ref, n_valid, n_out)

    @pl.when(pl.program_id(1) == 0)
    def _():
        ms = jnp.sum(ss_ref[...], axis=-1, keepdims=True) * (1.0 / width)
        rstd_ref[...] = jnp.broadcast_to(lax.rsqrt(ms + NORM_EPS), rstd_ref.shape)

    a = a_ref[...]
    rstd = rstd_ref[...]
    rstd = jnp.concatenate([rstd] * (o_ref.shape[1] // LANES), axis=1)
    g = jnp.dot(a, wg_ref[...], preferred_element_type=F32) * rstd
    u = jnp.dot(a, wu_ref[...], preferred_element_type=F32) * rstd
    o_ref[...] = (g * jax.nn.sigmoid(g) * u).astype(o_ref.dtype)


def _gate_up(a, ss, wg, wu, *, tm, tn, riders=()):
    m, kd = a.shape
    n = wg.shape[1]
    tm = min(tm, m)
    nj = n // tn
    in_specs = [pl.BlockSpec((tm, kd), lambda i, j: (i, 0)),
                pl.BlockSpec((tm, LANES), lambda i, j: (i, 0)),
                pl.BlockSpec((kd, tn), lambda i, j: (0, j)),
                pl.BlockSpec((kd, tn), lambda i, j: (0, j))]
    out_specs = [pl.BlockSpec((tm, tn), lambda i, j: (i, j))]
    out_shape = [jax.ShapeDtypeStruct((m, n), BF16)]
    r_in, r_out, r_shapes, meta = _rider_plumbing(riders, (m // tm) * nj, lambda i, j: i * nj + j)
    outs = pl.pallas_call(
        functools.partial(_gate_up_kernel, riders=meta, width=kd),
        grid=(m // tm, nj),
        in_specs=in_specs + r_in,
        out_specs=out_specs + r_out,
        out_shape=out_shape + r_shapes,
        scratch_shapes=[pltpu.VMEM((tm, LANES), F32)],
        compiler_params=_params("arbitrary", "arbitrary"),
        name="gate_up",
    )(a, ss, wg, wu, *[r.src for r in riders])
    return outs if riders else outs[0]


def _bdot(a, b):
    return jnp.dot(a.astype(BF16), b.astype(BF16), preferred_element_type=F32)


def _bdot_nt(a, b):
    return lax.dot_general(a.astype(BF16), b.astype(BF16), (((1,), (1,)), ((), ())),
                           preferred_element_type=F32)


def _bdot_tn(a, b):
    return lax.dot_general(a.astype(BF16), b.astype(BF16), (((0,), (0,)), ((), ())),
                           preferred_element_type=F32)


def _xdot(a, b):
    return jnp.dot(a, b, preferred_element_type=F32, precision=lax.Precision.HIGHEST)


def _silu(x):
    half = 0.5 * x
    return half + half * jnp.tanh(half)


def _gdn_kernel(q_ref, k_ref, v_ref, z_ref, ab_ref, cw_ref, alog_ref, dtb_ref, normw_ref,
                o_ref, tail_ref, s_ref, *, tb):
    t = pl.program_id(1)
    nh = DN_HEADS
    c = nh * HEAD_DIM
    pad = 8

    @pl.when(t == 0)
    def _():
        tail_ref[...] = jnp.zeros_like(tail_ref)
        s_ref[...] = jnp.zeros_like(s_ref)

    sections = (q_ref, k_ref, v_ref)

    ri = lax.broadcasted_iota(jnp.int32, (CHUNK, LANES), 0)
    lane = lax.broadcasted_iota(jnp.int32, (CHUNK, LANES), 1)
    ci = lane & (CHUNK - 1)
    first = lane < CHUNK
    first2 = lax.broadcasted_iota(jnp.int32, (2 * CHUNK, LANES), 1) < CHUNK
    top2 = lax.broadcasted_iota(jnp.int32, (2 * CHUNK, LANES), 0) < CHUNK
    incl = ri >= ci
    strict = ri > ci
    eye_f = jnp.where(ri == ci, 1.0, 0.0).astype(F32)
    ri_sq = lax.broadcasted_iota(jnp.int32, (CHUNK, CHUNK), 0)
    ci_sq = lax.broadcasted_iota(jnp.int32, (CHUNK, CHUNK), 1)
    tri_ones = jnp.where(ri_sq >= ci_sq, 1.0, 0.0).astype(F32)
    level_masks = []
    for lvl in range(CHUNK.bit_length() - 1):
        same_pair = (ri >> (lvl + 1)) == (ci >> (lvl + 1))
        level_masks.append(same_pair & (((ri >> lvl) & 1) == 1) & (((ci >> lvl) & 1) == 0))
    alog = alog_ref[...]
    dtb = dtb_ref[...]
    normw = normw_ref[...]
    scale = HEAD_DIM ** -0.5
    heads = range(nh)
    pairs = range(nh // 2)
    zeros_hd = jnp.zeros((CHUNK, HEAD_DIM), F32)

    def block_diag(m):
        mb = jnp.concatenate([m, m], axis=0).astype(BF16)
        return jnp.where(first2 == top2, mb, jnp.zeros_like(mb))

    def lanecol(x, h):
        return x[:, h:h + 1]

    ones_hd = jnp.ones((HEAD_DIM, HEAD_DIM), BF16)

    def lane_sum(x):
        return jnp.dot(x.astype(BF16), ones_hd, preferred_element_type=F32)

    group = 2
    members = range(group)

    def group_body(it, carry):
        base = it * (group * CHUNK)
        rows = [pl.ds(pl.multiple_of(base + j * CHUNK, CHUNK), CHUNK) for j in members]

        conv = {}

        def sl(j, section, h):
            if (0, section, h) not in conv:
                lo = section * c + h * HEAD_DIM
                w = cw_ref[:, lo:lo + HEAD_DIM]
                prev = tail_ref[:, lo:lo + HEAD_DIM]
                for m in members:
                    x = sections[section][rows[m], h * HEAD_DIM:(h + 1) * HEAD_DIM]
                    ext = jnp.concatenate([prev, x], axis=0)
                    acc = x * w[CONV_WIDTH - 1:CONV_WIDTH, :]
                    for shift in range(1, CONV_WIDTH):
                        tap = CONV_WIDTH - 1 - shift
                        acc = acc + pltpu.roll(ext, shift, axis=0)[pad:pad + CHUNK, :] * w[tap:tap + 1, :]
                    conv[m, section, h] = _silu(acc)
                    prev = x[CHUNK - pad:CHUNK, :]
                tail_ref[:, lo:lo + HEAD_DIM] = prev
            return conv[j, section, h]

        beta_all, gc_all, eg_all, tail_all, gl_all, gc_t = [], [], [], [], [], []
        for j in members:
            ab = ab_ref[rows[j], :]
            g = -jnp.exp(alog) * jax.nn.softplus(ab + dtb)
            beta_all.append(jax.nn.sigmoid(ab))
            gc = _xdot(tri_ones, g)
            gc_last = gc[CHUNK - 1:CHUNK, :]
            gc_all.append(gc)
            eg_all.append(jnp.exp(gc))
            tail_all.append(jnp.exp(gc_last - gc))
            gl_all.append(jnp.exp(gc_last))
            gc_t.append(jnp.concatenate([gc, pltpu.roll(gc, LANES - 1, axis=1)], axis=0).T)

        qn = [[None] * nh for _ in members]
        kn = [[None] * nh for _ in members]
        beta = [[None] * nh for _ in members]
        for j in members:
            for h in heads:
                qh = sl(j, 0, h)
                kh = sl(j, 1, h)
                qn[j][h] = qh * lax.rsqrt(lane_sum(qh * qh) + NORM_EPS) * scale
                kn[j][h] = kh * lax.rsqrt(lane_sum(kh * kh) + NORM_EPS)
                beta[j][h] = lanecol(beta_all[j], DN_HEADS + h)

        units = [(j, p) for j in members for p in pairs]
        qk, a_mat = {}, {}
        for j, p in units:
            h0, h1 = 2 * p, 2 * p + 1
            gc_col = jnp.where(first, lanecol(gc_all[j], h0), lanecol(gc_all[j], h1))
            decay = jnp.exp(jnp.where(incl, gc_col - gc_t[j][h0:h0 + 1, :], -jnp.inf))
            beta_p = jnp.where(first, beta[j][h0], beta[j][h1])
            lhs = jnp.concatenate([jnp.concatenate([qn[j][h0], qn[j][h1]], axis=1),
                                   jnp.concatenate([kn[j][h0], kn[j][h1]], axis=1)], axis=0)
            rhs = jnp.concatenate([jnp.concatenate([kn[j][h0], zeros_hd], axis=1),
                                   jnp.concatenate([zeros_hd, kn[j][h1]], axis=1)], axis=0)
            sc = _bdot_nt(lhs, rhs)
            qk[j, p] = sc[0:CHUNK] * decay
            a_mat[j, p] = jnp.where(strict, sc[CHUNK:2 * CHUNK] * beta_p * decay, 0.0)

        x_inv = {u: eye_f - jnp.where(level_masks[0], a_mat[u], 0.0) for u in units}
        for mask in level_masks[1:]:
            y = {u: jnp.dot(jnp.where(mask, a_mat[u], 0.0).astype(BF16), block_diag(x_inv[u]),
                            preferred_element_type=F32) for u in units}
            xy = {u: jnp.dot(x_inv[u].astype(BF16), block_diag(y[u]), preferred_element_type=F32) for u in units}
            x_inv = {u: x_inv[u] - xy[u] for u in units}

        uw = [[None] * nh for _ in members]
        for j, p in units:
            rhs = []
            for h in (2 * p, 2 * p + 1):
                kb_eg = kn[j][h] * (beta[j][h] * lanecol(eg_all[j], h))
                rhs.append(jnp.concatenate([sl(j, 2, h) * beta[j][h], kb_eg], axis=1))
            both = jnp.dot(block_diag(x_inv[j, p]), jnp.concatenate(rhs, axis=0).astype(BF16),
                           preferred_element_type=F32)
            uw[j][2 * p] = both[0:CHUNK]
            uw[j][2 * p + 1] = both[CHUNK:2 * CHUNK]

        for j in members:
            ws = []
            for h in heads:
                qd = qn[j][h] * lanecol(eg_all[j], h)
                ws.append(_bdot(jnp.concatenate([uw[j][h][:, HEAD_DIM:2 * HEAD_DIM], qd], axis=0), s_ref[h]))
            v_new = [uw[j][h][:, 0:HEAD_DIM] - ws[h][0:CHUNK] for h in heads]
            o_intra = [None] * nh
            for p in pairs:
                both = jnp.dot(block_diag(qk[j, p]),
                               jnp.concatenate([v_new[2 * p], v_new[2 * p + 1]], axis=0).astype(BF16),
                               preferred_element_type=F32)
                o_intra[2 * p] = both[0:CHUNK]
                o_intra[2 * p + 1] = both[CHUNK:2 * CHUNK]
            s_add = [_bdot_tn(kn[j][h] * lanecol(tail_all[j], h), v_new[h]) for h in heads]
            for h in heads:
                s_ref[h] = s_ref[h] * lanecol(gl_all[j], h) + s_add[h]
                o = ws[h][CHUNK:2 * CHUNK] + o_intra[h]
                zh = z_ref[rows[j], h * HEAD_DIM:(h + 1) * HEAD_DIM]
                o = o * lax.rsqrt(jnp.mean(o * o, axis=-1, keepdims=True) + NORM_EPS) * normw
                o_ref[rows[j], h * HEAD_DIM:(h + 1) * HEAD_DIM] = (
                    o * _silu(zh)).astype(o_ref.dtype)
        return carry

    lax.fori_loop(0, tb // (group * CHUNK), group_body, 0)


def _gdn(proj, ab, conv_w, alog_lane, dtb_lane, norm_w, *, batch, seq, tb):
    m = proj.shape[0]
    c = DN_HEADS * HEAD_DIM
    nt = seq // tb

    def col(section):
        return lambda b, t: (b * nt + t, section)

    lane_spec = pl.BlockSpec((1, LANES), lambda b, t: (0, 0))
    return pl.pallas_call(
        functools.partial(_gdn_kernel, tb=tb),
        grid=(batch, nt),
        in_specs=[pl.BlockSpec((tb, c), col(0)), pl.BlockSpec((tb, c), col(1)),
                  pl.BlockSpec((tb, c), col(2)), pl.BlockSpec((tb, c), col(3)),
                  pl.BlockSpec((tb, LANES), lambda b, t: (b * nt + t, 0)),
                  pl.BlockSpec((CONV_WIDTH, 3 * c), lambda b, t: (0, 0)),
                  lane_spec, lane_spec, lane_spec],
        out_specs=pl.BlockSpec((tb, c), lambda b, t: (b * nt + t, 0)),
        out_shape=jax.ShapeDtypeStruct((m, c), BF16),
        scratch_shapes=[pltpu.VMEM((8, 3 * c), F32), pltpu.VMEM((DN_HEADS, HEAD_DIM, HEAD_DIM), F32)],
        compiler_params=_params("parallel", "arbitrary"),
        name="gated_delta_rule",
    )(proj, proj, proj, proj, ab, conv_w, alog_lane, dtb_lane, norm_w)


def _swa_kernel(sinks_ref, q_ref, kp_ref, kc_ref, vp_ref, vc_ref, qw_ref, kw_ref, o_ref,
                bias_ref, bias_first_ref, sink_ref, *, nq):
    kvh = pl.program_id(1)
    i = pl.program_id(2)
    blk = WINDOW
    rows = SWA_GROUP * blk

    @pl.when(i == 0)
    def _():
        ri = lax.broadcasted_iota(jnp.int32, (rows, 2 * blk), 0)
        ki = lax.broadcasted_iota(jnp.int32, (rows, 2 * blk), 1)
        dist = (ri & (blk - 1)) + blk - ki
        head = kvh * SWA_GROUP + (ri >> (blk.bit_length() - 1))
        slope = jnp.exp2(-8.0 * (head + 1).astype(F32) / SWA_Q_HEADS)
        bias = jnp.where((dist >= 0) & (dist < WINDOW), -slope * dist.astype(F32), -jnp.inf)
        bias_ref[...] = bias
        bias_first_ref[...] = jnp.where(ki >= blk, bias, -jnp.inf)
        for g in range(SWA_GROUP):
            sink_ref[g * blk:(g + 1) * blk, :] = jnp.full((blk, LANES), sinks_ref[kvh * SWA_GROUP + g], F32)

    @pl.when(i == 1)
    def _():
        bias_first_ref[...] = bias_ref[...]

    def rms(x, w):
        return x * lax.rsqrt(jnp.mean(x * x, axis=-1, keepdims=True) + NORM_EPS) * w

    qw = qw_ref[...] * (HEAD_DIM ** -0.5)
    kw = kw_ref[...]
    subs = range(nq)
    kb = [rms(kp_ref[...], kw)] + [rms(kc_ref[s * blk:(s + 1) * blk, :], kw) for s in subs]
    vb = [vp_ref[...]] + [vc_ref[s * blk:(s + 1) * blk, :] for s in subs]
    q = [jnp.concatenate([rms(q_ref[s * blk:(s + 1) * blk, g * HEAD_DIM:(g + 1) * HEAD_DIM], qw)
                          for g in range(SWA_GROUP)], axis=0) for s in subs]
    sc = [_bdot_nt(q[s], jnp.concatenate([kb[s], kb[s + 1]], axis=0)) for s in subs]
    sink = sink_ref[...]
    p, denom = [], []
    for s in subs:
        x = sc[s] + (bias_first_ref[...] if s == 0 else bias_ref[...])
        mx = jnp.maximum(jnp.max(x, axis=-1, keepdims=True), sink)
        e = jnp.exp(x - jnp.concatenate([mx, mx], axis=1))
        p.append(e)
        denom.append(jnp.sum(e, axis=-1, keepdims=True) + jnp.exp(sink - mx))
    pv = [_bdot(p[s], jnp.concatenate([vb[s], vb[s + 1]], axis=0)) for s in subs]
    for s in subs:
        o = pv[s] / denom[s]
        for g in range(SWA_GROUP):
            o_ref[s * blk:(s + 1) * blk, g * HEAD_DIM:(g + 1) * HEAD_DIM] = (
                o[g * blk:(g + 1) * blk].astype(o_ref.dtype))


def _swa(proj, sinks, q_norm_w, k_norm_w, *, batch, seq, q_col, k_col, v_col, nq):
    m = proj.shape[0]
    blk = WINDOW
    nq = min(nq, seq // blk)
    steps = seq // (nq * blk)
    gw = SWA_GROUP * HEAD_DIM

    def cur(col):
        return lambda b, h, i: (b * steps + i, col + h)

    def prev(col):
        return lambda b, h, i: (b * steps * nq + jnp.maximum(i * nq - 1, 0), col + h)

    wspec = pl.BlockSpec((1, HEAD_DIM), lambda b, h, i: (0, 0))
    table = pltpu.VMEM((SWA_GROUP * blk, 2 * blk), F32)
    return pl.pallas_call(
        functools.partial(_swa_kernel, nq=nq),
        grid=(batch, SWA_KV_HEADS, steps),
        in_specs=[pl.BlockSpec(memory_space=pltpu.SMEM),
                  pl.BlockSpec((nq * blk, gw), lambda b, h, i: (b * steps + i, q_col // SWA_GROUP + h)),
                  pl.BlockSpec((blk, HEAD_DIM), prev(k_col)), pl.BlockSpec((nq * blk, HEAD_DIM), cur(k_col)),
                  pl.BlockSpec((blk, HEAD_DIM), prev(v_col)), pl.BlockSpec((nq * blk, HEAD_DIM), cur(v_col)),
                  wspec, wspec],
        out_specs=pl.BlockSpec((nq * blk, gw), lambda b, h, i: (b * steps + i, h)),
        out_shape=jax.ShapeDtypeStruct((m, SWA_Q_HEADS * HEAD_DIM), BF16),
        scratch_shapes=[table, table, pltpu.VMEM((SWA_GROUP * blk, LANES), F32)],
        compiler_params=_params("parallel", "parallel", "arbitrary"),
        name="sliding_window_gqa",
    )(sinks, proj, proj, proj, proj, proj, q_norm_w, k_norm_w)


def _pad_lanes(v):
    return jnp.pad(v.astype(F32), (0, LANES - v.shape[0])).reshape(1, LANES)


def _layer(x2, batch, seq, attn_norm_w, w_in, conv_w, a_log, dt_bias, dn_norm_w, q_norm_w, k_norm_w,
           sinks, w_out, ffn_norm_w, w_gate, w_up, w_down):
    m, d = x2.shape
    dn_w = DN_HEADS * HEAD_DIM
    q_w = SWA_Q_HEADS * HEAD_DIM
    kv_w = SWA_KV_HEADS * HEAD_DIM
    gates_at = 4 * dn_w
    n_gates = 2 * DN_HEADS

    w_main, w_ab = _split_w_in(w_in.T, gates_at=gates_at, n_gates=n_gates, tn=512)
    f = w_gate.shape[1]
    f_pad = -(-f // 512) * 512
    tm, tn_in, tn_out, tn_gu = min(1024, m), 1024, 512, 512

    in_riders = [_Rider(w_gate, d, f_pad), _Rider(w_out, d, w_out.shape[1])]
    ride_in = all(_rider_block_rows(r, (m // tm) * (w_main.shape[1] // tn_in)) for r in in_riders)
    out_rider = _Rider(w_up, d, f_pad)
    ride_out = _rider_block_rows(out_rider, (m // tm) * (w_out.shape[1] // tn_out)) is not None
    gu_rider = _Rider(w_down, f_pad, d)
    ride_gu = _rider_block_rows(gu_rider, (m // tm) * (f_pad // tn_gu)) is not None

    h = _rmsnorm(x2, attn_norm_w)
    if ride_in:
        proj, ab, wg, wo = _matmul(h, w_main, tm=tm, tn=tn_in, tk=d, out_dtype=F32, narrow_w=w_ab,
                                   riders=in_riders, name="in_proj")
    else:
        proj, ab = _matmul(h, w_main, tm=tm, tn=tn_in, tk=d, out_dtype=F32, narrow_w=w_ab, name="in_proj")
        wg = _cast_pad_cols(w_gate, f_pad, tr=256, name="cast_w_gate")
        wo = _cast_pad_cols(w_out, w_out.shape[1], tr=512, name="cast_w_out")

    o_dn = _gdn(proj, ab, conv_w, _pad_lanes(a_log), _pad_lanes(dt_bias), dn_norm_w.reshape(1, HEAD_DIM),
                batch=batch, seq=seq, tb=min(512, seq))
    sec = 4 * dn_w // HEAD_DIM
    o_sw = _swa(proj, sinks.astype(F32), q_norm_w.reshape(1, HEAD_DIM), k_norm_w.reshape(1, HEAD_DIM),
                batch=batch, seq=seq, q_col=sec, k_col=sec + q_w // HEAD_DIM,
                v_col=sec + (q_w + kv_w) // HEAD_DIM, nq=8)

    if ride_out:
        x1, xw, ss, wu = _matmul_pair(o_dn, o_sw, wo, x2, ffn_norm_w, tm=tm, tn=tn_out, riders=[out_rider],
                                      name="out_proj")
    else:
        x1, xw, ss = _matmul_pair(o_dn, o_sw, wo, x2, ffn_norm_w, tm=tm, tn=tn_out, name="out_proj")
        wu = _cast_pad_cols(w_up, f_pad, tr=256, name="cast_w_up")

    if ride_gu:
        act, wd = _gate_up(xw, ss, wg, wu, tm=tm, tn=tn_gu, riders=[gu_rider])
    else:
        act = _gate_up(xw, ss, wg, wu, tm=tm, tn=tn_gu)
        wd = _cast_pad_rows(w_down, f_pad, tr=256, name="cast_w_down")
    return _matmul(act, wd, tm=1024, tn=1024, tk=f_pad // 4, out_dtype=F32, residual=x1, name="down_proj")


def kernel(x, attn_norm_w, w_in, conv_w, a_log, dt_bias, dn_norm_w, q_norm_w, k_norm_w, sinks, w_out,
           ffn_norm_w, w_gate, w_up, w_down):
    batch, seq, d = x.shape
    x2 = x.reshape(batch * seq, d)
    for l in range(attn_norm_w.shape[0]):
        x2 = _layer(x2, batch, seq, attn_norm_w[l], w_in[l], conv_w[l], a_log[l], dt_bias[l], dn_norm_w[l],
                    q_norm_w[l], k_norm_w[l], sinks[l], w_out[l], ffn_norm_w[l], w_gate[l], w_up[l], w_down[l])
    return x2.reshape(batch, seq, d)
```

```python
import functools
from typing import NamedTuple

import jax
import jax.numpy as jnp
from jax import lax
from jax.experimental import pallas as pl
from jax.experimental.pallas import tpu as pltpu

NORM_EPS = 1e-6
HEAD_DIM = 128
DN_HEADS = 16
CHUNK = 64
CONV_WIDTH = 4
SWA_Q_HEADS = 16
SWA_KV_HEADS = 4
SWA_GROUP = SWA_Q_HEADS // SWA_KV_HEADS
WINDOW = 128
LANES = 128
VMEM_LIMIT_BYTES = 56 * 1024 * 1024

BF16 = jnp.bfloat16
F32 = jnp.float32


def _params(*semantics):
    return pltpu.CompilerParams(dimension_semantics=semantics, vmem_limit_bytes=VMEM_LIMIT_BYTES)


def _rmsnorm_kernel(x_ref, w_ref, o_ref):
    x = x_ref[...]
    ms = jnp.mean(x * x, axis=-1, keepdims=True)
    o_ref[...] = (x * lax.rsqrt(ms + NORM_EPS) * w_ref[...]).astype(o_ref.dtype)


def _rmsnorm(x, w, tm=512):
    m, d = x.shape
    tm = min(tm, m)
    return pl.pallas_call(
        _rmsnorm_kernel,
        grid=(m // tm,),
        in_specs=[pl.BlockSpec((tm, d), lambda i: (i, 0)), pl.BlockSpec((1, d), lambda i: (0, 0))],
        out_specs=pl.BlockSpec((tm, d), lambda i: (i, 0)),
        out_shape=jax.ShapeDtypeStruct((m, d), BF16),
        compiler_params=_params("parallel"),
        name="rmsnorm",
    )(x, w.reshape(1, d))


def _cast_cols_kernel(w_ref, o_ref):
    n_in = w_ref.shape[1]
    o_ref[:, 0:n_in] = w_ref[...].astype(o_ref.dtype)
    if o_ref.shape[1] > n_in:
        o_ref[:, n_in:] = jnp.zeros((o_ref.shape[0], o_ref.shape[1] - n_in), o_ref.dtype)


def _cast_pad_cols(w, n_out, *, tr, name):
    k, n = w.shape
    return pl.pallas_call(
        _cast_cols_kernel,
        grid=(k // tr,),
        in_specs=[pl.BlockSpec((tr, n), lambda i: (i, 0))],
        out_specs=pl.BlockSpec((tr, n_out), lambda i: (i, 0)),
        out_shape=jax.ShapeDtypeStruct((k, n_out), BF16),
        compiler_params=_params("parallel"),
        name=name,
    )(w)


def _cast_rows_kernel(w_ref, o_ref, *, n_valid):
    i = pl.program_id(0)

    @pl.when(i < n_valid)
    def _():
        o_ref[...] = w_ref[...].astype(o_ref.dtype)

    @pl.when(i >= n_valid)
    def _():
        o_ref[...] = jnp.zeros_like(o_ref)


def _cast_pad_rows(w, k_out, *, tr, name):
    k, n = w.shape
    n_valid = k // tr
    return pl.pallas_call(
        functools.partial(_cast_rows_kernel, n_valid=n_valid),
        grid=(k_out // tr,),
        in_specs=[pl.BlockSpec((tr, n), lambda i: (jnp.minimum(i, n_valid - 1), 0))],
        out_specs=pl.BlockSpec((tr, n), lambda i: (i, 0)),
        out_shape=jax.ShapeDtypeStruct((k_out, n), BF16),
        compiler_params=_params("parallel"),
        name=name,
    )(w)


def _split_main_kernel(a_ref, b_ref, main_ref, *, n_aligned, n_gates):
    j = pl.program_id(0)

    @pl.when(j < n_aligned)
    def _():
        main_ref[...] = a_ref[...].T.astype(BF16)

    @pl.when(j >= n_aligned)
    def _():
        src = jnp.concatenate([a_ref[n_gates:, :], b_ref[...]], axis=0)
        main_ref[...] = src.T.astype(BF16)


def _split_gates_kernel(g_ref, gates_ref, *, n_gates):
    lane = lax.broadcasted_iota(jnp.int32, gates_ref.shape, 1)
    gates_ref[...] = jnp.where(lane < n_gates, g_ref[...].T, 0.0).astype(BF16)


def _split_w_in(w_in_t, *, gates_at, n_gates, tn):
    n, k = w_in_t.shape
    n_main = n - n_gates
    main = pl.pallas_call(
        functools.partial(_split_main_kernel, n_aligned=gates_at // tn, n_gates=n_gates),
        grid=(n_main // tn,),
        in_specs=[pl.BlockSpec((tn, k), lambda j: (j, 0)),
                  pl.BlockSpec((n_gates, k), lambda j: ((j + 1) * (tn // n_gates), 0))],
        out_specs=pl.BlockSpec((k, tn), lambda j: (0, j)),
        out_shape=jax.ShapeDtypeStruct((k, n_main), BF16),
        compiler_params=_params("parallel"),
        name="split_w_in",
    )(w_in_t, w_in_t)
    gates = pl.pallas_call(
        functools.partial(_split_gates_kernel, n_gates=n_gates),
        grid=(1,),
        in_specs=[pl.BlockSpec((LANES, k), lambda j: (gates_at // LANES, 0))],
        out_specs=pl.BlockSpec((k, LANES), lambda j: (0, 0)),
        out_shape=jax.ShapeDtypeStruct((k, LANES), BF16),
        compiler_params=_params("arbitrary"),
        name="split_w_gates",
    )(w_in_t)
    return main, gates


class _Rider(NamedTuple):
    src: jax.Array
    rows_out: int
    cols_out: int


def _rider_block_rows(rider, steps):
    rows = rider.src.shape[0]
    br = 32
    while br <= rows:
        if rows % br == 0 and rider.rows_out % br == 0 and rider.rows_out // br <= steps:
            return br
        br *= 2
    return None


def _rider_plumbing(riders, steps, step_of):
    in_specs, out_specs, out_shapes, meta = [], [], [], []
    for r in riders:
        rows, cols = r.src.shape
        br = _rider_block_rows(r, steps)
        n_valid, n_out = rows // br, r.rows_out // br
        in_specs.append(pl.BlockSpec((br, cols), lambda *g, n=n_valid: (jnp.minimum(step_of(*g), n - 1), 0)))
        out_specs.append(pl.BlockSpec((br, r.cols_out), lambda *g, n=n_out: (jnp.minimum(step_of(*g), n - 1), 0)))
        out_shapes.append(jax.ShapeDtypeStruct((r.rows_out, r.cols_out), BF16))
        meta.append((n_valid, n_out))
    return in_specs, out_specs, out_shapes, tuple(meta)


def _rider_cast(step, src_ref, dst_ref, n_valid, n_out):
    cols = src_ref.shape[1]

    @pl.when(step < n_valid)
    def _():
        dst_ref[:, 0:cols] = src_ref[...].astype(dst_ref.dtype)
        if dst_ref.shape[1] > cols:
            dst_ref[:, cols:] = jnp.zeros((dst_ref.shape[0], dst_ref.shape[1] - cols), dst_ref.dtype)

    if n_out > n_valid:
        @pl.when(jnp.logical_and(step >= n_valid, step < n_out))
        def _():
            dst_ref[...] = jnp.zeros_like(dst_ref)


def _mm_kernel(a_ref, w_ref, *rest, nk, residual, narrow, riders=()):
    nr = len(riders)
    rest = list(rest)
    r_ref = rest.pop(0) if residual else None
    wn_ref = rest.pop(0) if narrow else None
    rider_in = [rest.pop(0) for _ in range(nr)]
    o_ref = rest.pop(0)
    on_ref = rest.pop(0) if narrow else None
    rider_out = rest
    if nr:
        step = pl.program_id(0) * pl.num_programs(1) + pl.program_id(1)
        for src_ref, dst_ref, (n_valid, n_out) in zip(rider_in, rider_out, riders):
            _rider_cast(step, src_ref, dst_ref, n_valid, n_out)
    if narrow:
        @pl.when(pl.program_id(1) == 0)
        def _():
            on_ref[...] = jnp.dot(a_ref[...], wn_ref[...], preferred_element_type=F32)
    if nk == 1:
        part = jnp.dot(a_ref[...], w_ref[...], preferred_element_type=F32)
        if residual:
            part = part + r_ref[...]
        o_ref[...] = part.astype(o_ref.dtype)
        return
    k = pl.program_id(2)

    @pl.when(k == 0)
    def _():
        if residual:
            o_ref[...] = r_ref[...]
        else:
            o_ref[...] = jnp.zeros_like(o_ref)

    o_ref[...] += jnp.dot(a_ref[...], w_ref[...], preferred_element_type=F32)


def _matmul(a, w, *, tm, tn, tk, out_dtype, residual=None, narrow_w=None, riders=(), name):
    m, kd = a.shape
    n = w.shape[1]
    tm = min(tm, m)
    nk = kd // tk
    nj = n // tn
    assert nk == 1 or (out_dtype == F32 and not riders and narrow_w is None)
    in_specs = [pl.BlockSpec((tm, tk), lambda i, j, k: (i, k)),
                pl.BlockSpec((tk, tn), lambda i, j, k: (k, j))]
    args = [a, w]
    out_specs = [pl.BlockSpec((tm, tn), lambda i, j, k: (i, j))]
    out_shape = [jax.ShapeDtypeStruct((m, n), out_dtype)]
    if residual is not None:
        in_specs.append(pl.BlockSpec((tm, tn), lambda i, j, k: (i, j)))
        args.append(residual)
    if narrow_w is not None:
        in_specs.append(pl.BlockSpec((kd, LANES), lambda i, j, k: (0, 0)))
        args.append(narrow_w)
        out_specs.append(pl.BlockSpec((tm, LANES), lambda i, j, k: (i, 0)))
        out_shape.append(jax.ShapeDtypeStruct((m, LANES), F32))
    r_in, r_out, r_shapes, meta = _rider_plumbing(riders, (m // tm) * nj, lambda i, j, k: i * nj + j)
    outs = pl.pallas_call(
        functools.partial(_mm_kernel, nk=nk, residual=residual is not None, narrow=narrow_w is not None,
                          riders=meta),
        grid=(m // tm, nj, nk),
        in_specs=in_specs + r_in,
        out_specs=out_specs + r_out,
        out_shape=out_shape + r_shapes,
        compiler_params=_params("arbitrary", "arbitrary", "arbitrary"),
        name=name,
    )(*args, *[r.src for r in riders])
    return outs if len(outs) > 1 else outs[0]


def _mm_pair_kernel(a1_ref, a2_ref, w_ref, r_ref, nw_ref, *rest, riders):
    nr = len(riders)
    o_ref, xw_ref, ss_ref = rest[nr:nr + 3]
    if nr:
        step = pl.program_id(0) * pl.num_programs(1) + pl.program_id(1)
        for src_ref, dst_ref, (n_valid, n_out) in zip(rest[:nr], rest[nr + 3:], riders):
            _rider_cast(step, src_ref, dst_ref, n_valid, n_out)
    k1 = a1_ref.shape[1]
    acc = jnp.dot(a1_ref[...], w_ref[0:k1, :], preferred_element_type=F32)
    acc = acc + jnp.dot(a2_ref[...], w_ref[k1:, :], preferred_element_type=F32)
    x = acc + r_ref[...]
    o_ref[...] = x
    xw_ref[...] = (x * nw_ref[...]).astype(xw_ref.dtype)
    sq = x * x
    part = sq[:, 0:LANES]
    for c in range(1, sq.shape[1] // LANES):
        part = part + sq[:, c * LANES:(c + 1) * LANES]

    @pl.when(pl.program_id(1) == 0)
    def _():
        ss_ref[...] = part

    @pl.when(pl.program_id(1) > 0)
    def _():
        ss_ref[...] += part


def _matmul_pair(a1, a2, w, residual, norm_w, *, tm, tn, riders=(), name):
    m, k1 = a1.shape
    k2 = a2.shape[1]
    n = w.shape[1]
    tm = min(tm, m)
    nj = n // tn
    r_in, r_out, r_shapes, meta = _rider_plumbing(riders, (m // tm) * nj, lambda i, j: i * nj + j)
    return pl.pallas_call(
        functools.partial(_mm_pair_kernel, riders=meta),
        grid=(m // tm, nj),
        in_specs=[pl.BlockSpec((tm, k1), lambda i, j: (i, 0)),
                  pl.BlockSpec((tm, k2), lambda i, j: (i, 0)),
                  pl.BlockSpec((k1 + k2, tn), lambda i, j: (0, j)),
                  pl.BlockSpec((tm, tn), lambda i, j: (i, j)),
                  pl.BlockSpec((1, tn), lambda i, j: (0, j))] + r_in,
        out_specs=[pl.BlockSpec((tm, tn), lambda i, j: (i, j)),
                   pl.BlockSpec((tm, tn), lambda i, j: (i, j)),
                   pl.BlockSpec((tm, LANES), lambda i, j: (i, 0))] + r_out,
        out_shape=[jax.ShapeDtypeStruct((m, n), F32), jax.ShapeDtypeStruct((m, n), BF16),
                   jax.ShapeDtypeStruct((m, LANES), F32)] + r_shapes,
        compiler_params=_params("arbitrary", "arbitrary"),
        name=name,
    )(a1, a2, w, residual, norm_w.reshape(1, n), *[r.src for r in riders])


def _gate_up_kernel(a_ref, ss_ref, wg_ref, wu_ref, *rest, riders, width):
    nr = len(riders)
    o_ref = rest[nr]
    rstd_ref = rest[-1]
    if nr:
        step = pl.program_id(0) * pl.num_programs(1) + pl.program_id(1)
        for src_ref, dst_ref, (n_valid, n_out) in zip(rest[:nr], rest[nr + 1:-1], riders):
            _rider_cast(step, src_ref, dst_ref, n_valid, n_out)

    @pl.when(pl.program_id(1) == 0)
    def _():
        ms = jnp.sum(ss_ref[...], axis=-1, keepdims=True) * (1.0 / width)
        rstd_ref[...] = jnp.broadcast_to(lax.rsqrt(ms + NORM_EPS), rstd_ref.shape)

    a = a_ref[...]
    rstd = rstd_ref[...]
    rstd = jnp.concatenate([rstd] * (o_ref.shape[1] // LANES), axis=1)
    g = jnp.dot(a, wg_ref[...], preferred_element_type=F32) * rstd
    u = jnp.dot(a, wu_ref[...], preferred_element_type=F32) * rstd
    o_ref[...] = (g * jax.nn.sigmoid(g) * u).astype(o_ref.dtype)


def _gate_up(a, ss, wg, wu, *, tm, tn, riders=()):
    m, kd = a.shape
    n = wg.shape[1]
    tm = min(tm, m)
    nj = n // tn
    in_specs = [pl.BlockSpec((tm, kd), lambda i, j: (i, 0)),
                pl.BlockSpec((tm, LANES), lambda i, j: (i, 0)),
                pl.BlockSpec((kd, tn), lambda i, j: (0, j)),
                pl.BlockSpec((kd, tn), lambda i, j: (0, j))]
    out_specs = [pl.BlockSpec((tm, tn), lambda i, j: (i, j))]
    out_shape = [jax.ShapeDtypeStruct((m, n), BF16)]
    r_in, r_out, r_shapes, meta = _rider_plumbing(riders, (m // tm) * nj, lambda i, j: i * nj + j)
    outs = pl.pallas_call(
        functools.partial(_gate_up_kernel, riders=meta, width=kd),
        grid=(m // tm, nj),
        in_specs=in_specs + r_in,
        out_specs=out_specs + r_out,
        out_shape=out_shape + r_shapes,
        scratch_shapes=[pltpu.VMEM((tm, LANES), F32)],
        compiler_params=_params("arbitrary", "arbitrary"),
        name="gate_up",
    )(a, ss, wg, wu, *[r.src for r in riders])
    return outs if riders else outs[0]


def _bdot(a, b):
    return jnp.dot(a.astype(BF16), b.astype(BF16), preferred_element_type=F32)


def _bdot_nt(a, b):
    return lax.dot_general(a.astype(BF16), b.astype(BF16), (((1,), (1,)), ((), ())),
                           preferred_element_type=F32)


def _bdot_tn(a, b):
    return lax.dot_general(a.astype(BF16), b.astype(BF16), (((0,), (0,)), ((), ())),
                           preferred_element_type=F32)


def _xdot(a, b):
    return jnp.dot(a, b, preferred_element_type=F32, precision=lax.Precision.HIGHEST)


def _silu(x):
    half = 0.5 * x
    return half + half * jnp.tanh(half)


def _gdn_kernel(q_ref, k_ref, v_ref, z_ref, ab_ref, cw_ref, alog_ref, dtb_ref, normw_ref,
                o_ref, tail_ref, s_ref, *, tb):
    t = pl.program_id(1)
    nh = DN_HEADS
    c = nh * HEAD_DIM
    pad = 8

    @pl.when(t == 0)
    def _():
        tail_ref[...] = jnp.zeros_like(tail_ref)
        s_ref[...] = jnp.zeros_like(s_ref)

    sections = (q_ref, k_ref, v_ref)

    ri = lax.broadcasted_iota(jnp.int32, (CHUNK, LANES), 0)
    lane = lax.broadcasted_iota(jnp.int32, (CHUNK, LANES), 1)
    ci = lane & (CHUNK - 1)
    first = lane < CHUNK
    first2 = lax.broadcasted_iota(jnp.int32, (2 * CHUNK, LANES), 1) < CHUNK
    top2 = lax.broadcasted_iota(jnp.int32, (2 * CHUNK, LANES), 0) < CHUNK
    incl = ri >= ci
    strict = ri > ci
    eye_f = jnp.where(ri == ci, 1.0, 0.0).astype(F32)
    ri_sq = lax.broadcasted_iota(jnp.int32, (CHUNK, CHUNK), 0)
    ci_sq = lax.broadcasted_iota(jnp.int32, (CHUNK, CHUNK), 1)
    tri_ones = jnp.where(ri_sq >= ci_sq, 1.0, 0.0).astype(F32)
    level_masks = []
    for lvl in range(CHUNK.bit_length() - 1):
        same_pair = (ri >> (lvl + 1)) == (ci >> (lvl + 1))
        level_masks.append(same_pair & (((ri >> lvl) & 1) == 1) & (((ci >> lvl) & 1) == 0))
    alog = alog_ref[...]
    dtb = dtb_ref[...]
    normw = normw_ref[...]
    scale = HEAD_DIM ** -0.5
    heads = range(nh)
    pairs = range(nh // 2)
    zeros_hd = jnp.zeros((CHUNK, HEAD_DIM), F32)

    def block_diag(m):
        mb = jnp.concatenate([m, m], axis=0).astype(BF16)
        return jnp.where(first2 == top2, mb, jnp.zeros_like(mb))

    def lanecol(x, h):
        return x[:, h:h + 1]

    ones_hd = jnp.ones((HEAD_DIM, HEAD_DIM), BF16)

    def lane_sum(x):
        return jnp.dot(x.astype(BF16), ones_hd, preferred_element_type=F32)

    group = 2
    members = range(group)

    def group_body(it, carry):
        base = it * (group * CHUNK)
        rows = [pl.ds(pl.multiple_of(base + j * CHUNK, CHUNK), CHUNK) for j in members]

        conv = {}

        def sl(j, section, h):
            if (0, section, h) not in conv:
                lo = section * c + h * HEAD_DIM
                w = cw_ref[:, lo:lo + HEAD_DIM]
                prev = tail_ref[:, lo:lo + HEAD_DIM]
                for m in members:
                    x = sections[section][rows[m], h * HEAD_DIM:(h + 1) * HEAD_DIM]
                    ext = jnp.concatenate([prev, x], axis=0)
                    acc = x * w[CONV_WIDTH - 1:CONV_WIDTH, :]
                    for shift in range(1, CONV_WIDTH):
                        tap = CONV_WIDTH - 1 - shift
                        acc = acc + pltpu.roll(ext, shift, axis=0)[pad:pad + CHUNK, :] * w[tap:tap + 1, :]
                    conv[m, section, h] = _silu(acc)
                    prev = x[CHUNK - pad:CHUNK, :]
                tail_ref[:, lo:lo + HEAD_DIM] = prev
            return conv[j, section, h]

        beta_all, gc_all, eg_all, tail_all, gl_all, gc_t = [], [], [], [], [], []
        for j in members:
            ab = ab_ref[rows[j], :]
            g = -jnp.exp(alog) * jax.nn.softplus(ab + dtb)
            beta_all.append(jax.nn.sigmoid(ab))
            gc = _xdot(tri_ones, g)
            gc_last = gc[CHUNK - 1:CHUNK, :]
            gc_all.append(gc)
            eg_all.append(jnp.exp(gc))
            tail_all.append(jnp.exp(gc_last - gc))
            gl_all.append(jnp.exp(gc_last))
            gc_t.append(jnp.concatenate([gc, pltpu.roll(gc, LANES - 1, axis=1)], axis=0).T)

        qn = [[None] * nh for _ in members]
        kn = [[None] * nh for _ in members]
        beta = [[None] * nh for _ in members]
        for j in members:
            for h in heads:
                qh = sl(j, 0, h)
                kh = sl(j, 1, h)
                qn[j][h] = qh * lax.rsqrt(lane_sum(qh * qh) + NORM_EPS) * scale
                kn[j][h] = kh * lax.rsqrt(lane_sum(kh * kh) + NORM_EPS)
                beta[j][h] = lanecol(beta_all[j], DN_HEADS + h)

        units = [(j, p) for j in members for p in pairs]
        qk, a_mat = {}, {}
        for j, p in units:
            h0, h1 = 2 * p, 2 * p + 1
            gc_col = jnp.where(first, lanecol(gc_all[j], h0), lanecol(gc_all[j], h1))
            decay = jnp.exp(jnp.where(incl, gc_col - gc_t[j][h0:h0 + 1, :], -jnp.inf))
            beta_p = jnp.where(first, beta[j][h0], beta[j][h1])
            lhs = jnp.concatenate([jnp.concatenate([qn[j][h0], qn[j][h1]], axis=1),
                                   jnp.concatenate([kn[j][h0], kn[j][h1]], axis=1)], axis=0)
            rhs = jnp.concatenate([jnp.concatenate([kn[j][h0], zeros_hd], axis=1),
                                   jnp.concatenate([zeros_hd, kn[j][h1]], axis=1)], axis=0)
            sc = _bdot_nt(lhs, rhs)
            qk[j, p] = sc[0:CHUNK] * decay
            a_mat[j, p] = jnp.where(strict, sc[CHUNK:2 * CHUNK] * beta_p * decay, 0.0)

        x_inv = {u: eye_f - jnp.where(level_masks[0], a_mat[u], 0.0) for u in units}
        for mask in level_masks[1:]:
            y = {u: jnp.dot(jnp.where(mask, a_mat[u], 0.0).astype(BF16), block_diag(x_inv[u]),
                            preferred_element_type=F32) for u in units}
            xy = {u: jnp.dot(x_inv[u].astype(BF16), block_diag(y[u]), preferred_element_type=F32) for u in units}
            x_inv = {u: x_inv[u] - xy[u] for u in units}

        uw = [[None] * nh for _ in members]
        for j, p in units:
            rhs = []
            for h in (2 * p, 2 * p + 1):
                kb_eg = kn[j][h] * (beta[j][h] * lanecol(eg_all[j], h))
                rhs.append(jnp.concatenate([sl(j, 2, h) * beta[j][h], kb_eg], axis=1))
            both = jnp.dot(block_diag(x_inv[j, p]), jnp.concatenate(rhs, axis=0).astype(BF16),
                           preferred_element_type=F32)
            uw[j][2 * p] = both[0:CHUNK]
            uw[j][2 * p + 1] = both[CHUNK:2 * CHUNK]

        for j in members:
            ws = []
            for h in heads:
                qd = qn[j][h] * lanecol(eg_all[j], h)
                ws.append(_bdot(jnp.concatenate([uw[j][h][:, HEAD_DIM:2 * HEAD_DIM], qd], axis=0), s_ref[h]))
            v_new = [uw[j][h][:, 0:HEAD_DIM] - ws[h][0:CHUNK] for h in heads]
            o_intra = [None] * nh
            for p in pairs:
                both = jnp.dot(block_diag(qk[j, p]),
                               jnp.concatenate([v_new[2 * p], v_new[2 * p + 1]], axis=0).astype(BF16),
                               preferred_element_type=F32)
                o_intra[2 * p] = both[0:CHUNK]
                o_intra[2 * p + 1] = both[CHUNK:2 * CHUNK]
            s_add = [_bdot_tn(kn[j][h] * lanecol(tail_all[j], h), v_new[h]) for h in heads]
            for h in heads:
                s_ref[h] = s_ref[h] * lanecol(gl_all[j], h) + s_add[h]
                o = ws[h][CHUNK:2 * CHUNK] + o_intra[h]
                zh = z_ref[rows[j], h * HEAD_DIM:(h + 1) * HEAD_DIM]
                o = o * lax.rsqrt(jnp.mean(o * o, axis=-1, keepdims=True) + NORM_EPS) * normw
                o_ref[rows[j], h * HEAD_DIM:(h + 1) * HEAD_DIM] = (
                    o * _silu(zh)).astype(o_ref.dtype)
        return carry

    lax.fori_loop(0, tb // (group * CHUNK), group_body, 0)


def _gdn(proj, ab, conv_w, alog_lane, dtb_lane, norm_w, *, batch, seq, tb):
    m = proj.shape[0]
    c = DN_HEADS * HEAD_DIM
    nt = seq // tb

    def col(section):
        return lambda b, t: (b * nt + t, section)

    lane_spec = pl.BlockSpec((1, LANES), lambda b, t: (0, 0))
    return pl.pallas_call(
        functools.partial(_gdn_kernel, tb=tb),
        grid=(batch, nt),
        in_specs=[pl.BlockSpec((tb, c), col(0)), pl.BlockSpec((tb, c), col(1)),
                  pl.BlockSpec((tb, c), col(2)), pl.BlockSpec((tb, c), col(3)),
                  pl.BlockSpec((tb, LANES), lambda b, t: (b * nt + t, 0)),
                  pl.BlockSpec((CONV_WIDTH, 3 * c), lambda b, t: (0, 0)),
                  lane_spec, lane_spec, lane_spec],
        out_specs=pl.BlockSpec((tb, c), lambda b, t: (b * nt + t, 0)),
        out_shape=jax.ShapeDtypeStruct((m, c), BF16),
        scratch_shapes=[pltpu.VMEM((8, 3 * c), F32), pltpu.VMEM((DN_HEADS, HEAD_DIM, HEAD_DIM), F32)],
        compiler_params=_params("parallel", "arbitrary"),
        name="gated_delta_rule",
    )(proj, proj, proj, proj, ab, conv_w, alog_lane, dtb_lane, norm_w)


def _swa_kernel(sinks_ref, q_ref, kp_ref, kc_ref, vp_ref, vc_ref, qw_ref, kw_ref, o_ref,
                bias_ref, bias_first_ref, sink_ref, *, nq):
    kvh = pl.program_id(1)
    i = pl.program_id(2)
    blk = WINDOW
    rows = SWA_GROUP * blk

    @pl.when(i == 0)
    def _():
        ri = lax.broadcasted_iota(jnp.int32, (rows, 2 * blk), 0)
        ki = lax.broadcasted_iota(jnp.int32, (rows, 2 * blk), 1)
        dist = (ri & (blk - 1)) + blk - ki
        head = kvh * SWA_GROUP + (ri >> (blk.bit_length() - 1))
        slope = jnp.exp2(-8.0 * (head + 1).astype(F32) / SWA_Q_HEADS)
        bias = jnp.where((dist >= 0) & (dist < WINDOW), -slope * dist.astype(F32), -jnp.inf)
        bias_ref[...] = bias
        bias_first_ref[...] = jnp.where(ki >= blk, bias, -jnp.inf)
        for g in range(SWA_GROUP):
            sink_ref[g * blk:(g + 1) * blk, :] = jnp.full((blk, LANES), sinks_ref[kvh * SWA_GROUP + g], F32)

    @pl.when(i == 1)
    def _():
        bias_first_ref[...] = bias_ref[...]

    def rms(x, w):
        return x * lax.rsqrt(jnp.mean(x * x, axis=-1, keepdims=True) + NORM_EPS) * w

    qw = qw_ref[...] * (HEAD_DIM ** -0.5)
    kw = kw_ref[...]
    subs = range(nq)
    kb = [rms(kp_ref[...], kw)] + [rms(kc_ref[s * blk:(s + 1) * blk, :], kw) for s in subs]
    vb = [vp_ref[...]] + [vc_ref[s * blk:(s + 1) * blk, :] for s in subs]
    q = [jnp.concatenate([rms(q_ref[s * blk:(s + 1) * blk, g * HEAD_DIM:(g + 1) * HEAD_DIM], qw)
                          for g in range(SWA_GROUP)], axis=0) for s in subs]
    sc = [_bdot_nt(q[s], jnp.concatenate([kb[s], kb[s + 1]], axis=0)) for s in subs]
    sink = sink_ref[...]
    p, denom = [], []
    for s in subs:
        x = sc[s] + (bias_first_ref[...] if s == 0 else bias_ref[...])
        mx = jnp.maximum(jnp.max(x, axis=-1, keepdims=True), sink)
        e = jnp.exp(x - jnp.concatenate([mx, mx], axis=1))
        p.append(e)
        denom.append(jnp.sum(e, axis=-1, keepdims=True) + jnp.exp(sink - mx))
    pv = [_bdot(p[s], jnp.concatenate([vb[s], vb[s + 1]], axis=0)) for s in subs]
    for s in subs:
        o = pv[s] / denom[s]
        for g in range(SWA_GROUP):
            o_ref[s * blk:(s + 1) * blk, g * HEAD_DIM:(g + 1) * HEAD_DIM] = (
                o[g * blk:(g + 1) * blk].astype(o_ref.dtype))


def _swa(proj, sinks, q_norm_w, k_norm_w, *, batch, seq, q_col, k_col, v_col, nq):
    m = proj.shape[0]
    blk = WINDOW
    nq = min(nq, seq // blk)
    steps = seq // (nq * blk)
    gw = SWA_GROUP * HEAD_DIM

    def cur(col):
        return lambda b, h, i: (b * steps + i, col + h)

    def prev(col):
        return lambda b, h, i: (b * steps * nq + jnp.maximum(i * nq - 1, 0), col + h)

    wspec = pl.BlockSpec((1, HEAD_DIM), lambda b, h, i: (0, 0))
    table = pltpu.VMEM((SWA_GROUP * blk, 2 * blk), F32)
    return pl.pallas_call(
        functools.partial(_swa_kernel, nq=nq),
        grid=(batch, SWA_KV_HEADS, steps),
        in_specs=[pl.BlockSpec(memory_space=pltpu.SMEM),
                  pl.BlockSpec((nq * blk, gw), lambda b, h, i: (b * steps + i, q_col // SWA_GROUP + h)),
                  pl.BlockSpec((blk, HEAD_DIM), prev(k_col)), pl.BlockSpec((nq * blk, HEAD_DIM), cur(k_col)),
                  pl.BlockSpec((blk, HEAD_DIM), prev(v_col)), pl.BlockSpec((nq * blk, HEAD_DIM), cur(v_col)),
                  wspec, wspec],
        out_specs=pl.BlockSpec((nq * blk, gw), lambda b, h, i: (b * steps + i, h)),
        out_shape=jax.ShapeDtypeStruct((m, SWA_Q_HEADS * HEAD_DIM), BF16),
        scratch_shapes=[table, table, pltpu.VMEM((SWA_GROUP * blk, LANES), F32)],
        compiler_params=_params("parallel", "parallel", "arbitrary"),
        name="sliding_window_gqa",
    )(sinks, proj, proj, proj, proj, proj, q_norm_w, k_norm_w)


def _pad_lanes(v):
    return jnp.pad(v.astype(F32), (0, LANES - v.shape[0])).reshape(1, LANES)


def _layer(x2, batch, seq, attn_norm_w, w_in, conv_w, a_log, dt_bias, dn_norm_w, q_norm_w, k_norm_w,
           sinks, w_out, ffn_norm_w, w_gate, w_up, w_down):
    m, d = x2.shape
    dn_w = DN_HEADS * HEAD_DIM
    q_w = SWA_Q_HEADS * HEAD_DIM
    kv_w = SWA_KV_HEADS * HEAD_DIM
    gates_at = 4 * dn_w
    n_gates = 2 * DN_HEADS

    w_main, w_ab = _split_w_in(w_in.T, gates_at=gates_at, n_gates=n_gates, tn=512)
    f = w_gate.shape[1]
    f_pad = -(-f // 512) * 512
    tm, tn_in, tn_out, tn_gu = min(1024, m), 1024, 512, 512

    in_riders = [_Rider(w_gate, d, f_pad), _Rider(w_out, d, w_out.shape[1])]
    ride_in = all(_rider_block_rows(r, (m // tm) * (w_main.shape[1] // tn_in)) for r in in_riders)
    out_rider = _Rider(w_up, d, f_pad)
    ride_out = _rider_block_rows(out_rider, (m // tm) * (w_out.shape[1] // tn_out)) is not None
    gu_rider = _Rider(w_down, f_pad, d)
    ride_gu = _rider_block_rows(gu_rider, (m // tm) * (f_pad // tn_gu)) is not None

    h = _rmsnorm(x2, attn_norm_w)
    if ride_in:
        proj, ab, wg, wo = _matmul(h, w_main, tm=tm, tn=tn_in, tk=d, out_dtype=F32, narrow_w=w_ab,
                                   riders=in_riders, name="in_proj")
    else:
        proj, ab = _matmul(h, w_main, tm=tm, tn=tn_in, tk=d, out_dtype=F32, narrow_w=w_ab, name="in_proj")
        wg = _cast_pad_cols(w_gate, f_pad, tr=256, name="cast_w_gate")
        wo = _cast_pad_cols(w_out, w_out.shape[1], tr=512, name="cast_w_out")

    o_dn = _gdn(proj, ab, conv_w, _pad_lanes(a_log), _pad_lanes(dt_bias), dn_norm_w.reshape(1, HEAD_DIM),
                batch=batch, seq=seq, tb=min(512, seq))
    sec = 4 * dn_w // HEAD_DIM
    o_sw = _swa(proj, sinks.astype(F32), q_norm_w.reshape(1, HEAD_DIM), k_norm_w.reshape(1, HEAD_DIM),
                batch=batch, seq=seq, q_col=sec, k_col=sec + q_w // HEAD_DIM,
                v_col=sec + (q_w + kv_w) // HEAD_DIM, nq=8)

    if ride_out:
        x1, xw, ss, wu = _matmul_pair(o_dn, o_sw, wo, x2, ffn_norm_w, tm=tm, tn=tn_out, riders=[out_rider],
                                      name="out_proj")
    else:
        x1, xw, ss = _matmul_pair(o_dn, o_sw, wo, x2, ffn_norm_w, tm=tm, tn=tn_out, name="out_proj")
        wu = _cast_pad_cols(w_up, f_pad, tr=256, name="cast_w_up")

    if ride_gu:
        act, wd = _gate_up(xw, ss, wg, wu, tm=tm, tn=tn_gu, riders=[gu_rider])
    else:
        act = _gate_up(xw, ss, wg, wu, tm=tm, tn=tn_gu)
        wd = _cast_pad_rows(w_down, f_pad, tr=256, name="cast_w_down")
    return _matmul(act, wd, tm=1024, tn=1024, tk=f_pad // 4, out_dtype=F32, residual=x1, name="down_proj")


def kernel(x, attn_norm_w, w_in, conv_w, a_log, dt_bias, dn_norm_w, q_norm_w, k_norm_w, sinks, w_out,
           ffn_norm_w, w_gate, w_up, w_down):
    batch, seq, d = x.shape
    x2 = x.reshape(batch * seq, d)
    for l in range(attn_norm_w.shape[0]):
        x2 = _layer(x2, batch, seq, attn_norm_w[l], w_in[l], conv_w[l], a_log[l], dt_bias[l], dn_norm_w[l],
                    q_norm_w[l], k_norm_w[l], sinks[l], w_out[l], ffn_norm_w[l], w_gate[l], w_up[l], w_down[l])
    return x2.reshape(batch, seq, d)
```
